```python
import math
import jax, jax.numpy as jnp
from jax import lax
import numpy as np

D_MODEL = 1024
BATCH = 16
SEQ = 2048
DEPTH = 4

D_MIX = D_MODEL
D_ATTN = D_MIX // 2
D_REC = D_MIX - D_ATTN
HEAD_DIM = 64
N_ATTN_HEADS = D_ATTN // HEAD_DIM
N_REC_BLOCKS = 8
REC_BLOCK = D_REC // N_REC_BLOCKS
CONV_WIDTH = 4
LRU_C = 8.0
DILATED_PATTERNS = ((128, 1), (512, 4), (2048, 16))
ROPE_THETA = 10000.0
D_FF = 256 * ((8 * D_MODEL // 3 + 255) // 256)
D_IN_PROJ = 3 * D_ATTN + 2 * D_REC
EPS = 1e-6

kernel_name = "hymba_rglru_dilated_attn_macaron"


def rms_norm(x, g):
    xf = x.astype(jnp.float32)
    var = jnp.mean(xf * xf, axis=-1, keepdims=True)
    return (xf * lax.rsqrt(var + EPS) * g.astype(jnp.float32)).astype(x.dtype)


def swiglu_ffn(x, g, w_in, w_out):
    h = rms_norm(x, g) @ w_in
    gate, up = jnp.split(h, 2, axis=-1)
    return (jax.nn.silu(gate) * up) @ w_out


def rope_tables(seq):
    pos = jnp.arange(seq, dtype=jnp.float32)
    inv = ROPE_THETA ** (-jnp.arange(0, HEAD_DIM, 2, dtype=jnp.float32) / HEAD_DIM)
    ang = pos[:, None] * inv[None, :]
    return jnp.cos(ang), jnp.sin(ang)


def apply_rope(t, cos, sin):
    tf = t.astype(jnp.float32)
    half = HEAD_DIM // 2
    t1, t2 = tf[..., :half], tf[..., half:]
    c = cos[None, :, None, :]
    s = sin[None, :, None, :]
    return jnp.concatenate([t1 * c - t2 * s, t2 * c + t1 * s], axis=-1)


def dilated_band_attention(q, k, v, window, dilation):
    B, S, H, Dh = q.shape
    steps = window // dilation
    L = S // dilation
    nb = -(-L // steps)
    Lp = nb * steps

    def to_classes(t):
        return t.reshape(B, L, dilation, H, Dh).transpose(0, 2, 3, 1, 4)

    qc, kc, vc = to_classes(q), to_classes(k), to_classes(v)
    qc = jnp.pad(qc, ((0, 0), (0, 0), (0, 0), (0, Lp - L), (0, 0)))
    kc = jnp.pad(kc, ((0, 0), (0, 0), (0, 0), (steps, Lp - L), (0, 0)))
    vc = jnp.pad(vc, ((0, 0), (0, 0), (0, 0), (steps, Lp - L), (0, 0)))

    def band_keys(t):
        prev = t[:, :, :, :Lp].reshape(B, dilation, H, nb, steps, Dh)
        cur = t[:, :, :, steps:].reshape(B, dilation, H, nb, steps, Dh)
        return jnp.concatenate([prev, cur], axis=-2)

    q_blk = qc.reshape(B, dilation, H, nb, steps, Dh)
    k_blk, v_blk = band_keys(kc), band_keys(vc)

    scores = jnp.einsum('bdhnqc,bdhnkc->bdhnqk', q_blk, k_blk,
                        preferred_element_type=jnp.float32) / math.sqrt(Dh)
    qi = jnp.arange(steps)[:, None]
    kj = jnp.arange(2 * steps)[None, :]
    dist = qi + steps - kj
    key_idx = jnp.arange(nb)[:, None, None] * steps - steps + kj[None]
    valid = (dist >= 0)[None] & (dist <= steps)[None] & (key_idx >= 0)
    scores = jnp.where(valid, scores, -jnp.inf)
    m = jnp.max(scores, axis=-1, keepdims=True)
    p = jnp.exp(scores - m)
    s = jnp.sum(p, axis=-1, keepdims=True)
    num = jnp.einsum('bdhnqk,bdhnkc->bdhnqc', p, v_blk.astype(jnp.float32))

    def from_classes(t):
        c = t.shape[-1]
        t = t.reshape(B, dilation, H, Lp, c)[:, :, :, :L]
        return t.transpose(0, 3, 1, 2, 4).reshape(B, S, H, c)

    return from_classes(num), from_classes(m), from_classes(s)


def dilated_attention_group(q, k, v):
    nums, ms, ss = [], [], []
    for window, dilation in DILATED_PATTERNS:
        n_, m_, s_ = dilated_band_attention(q, k, v, window, dilation)
        nums.append(n_); ms.append(m_); ss.append(s_)
    m_all = jnp.maximum(jnp.maximum(ms[0], ms[1]), ms[2])
    ws = [jnp.exp(m_ - m_all) for m_ in ms]
    numer = nums[0] * ws[0] + nums[1] * ws[1] + nums[2] * ws[2]
    denom = ss[0] * ws[0] + ss[1] * ws[1] + ss[2] * ws[2]
    return numer / denom


def causal_depthwise_conv(x, w, b):
    C = x.shape[-1]
    out = lax.conv_general_dilated(
        x, w[:, None, :].astype(x.dtype), window_strides=(1,),
        padding=[(CONV_WIDTH - 1, 0)], dimension_numbers=('NWC', 'WIO', 'NWC'),
        feature_group_count=C)
    return out + b


def block_diag_linear(x, w, b):
    B, S, _ = x.shape
    xb = x.reshape(B, S, N_REC_BLOCKS, REC_BLOCK)
    return jnp.einsum('bsgi,gij->bsgj', xb, w).reshape(B, S, D_REC) + b


def rglru_group(xb, gb, conv_w, conv_b, w_a, b_a, w_x, b_x, lam):
    xr = causal_depthwise_conv(xb, conv_w, conv_b).astype(jnp.float32)
    r = jax.nn.sigmoid(block_diag_linear(xr, w_a.astype(jnp.float32), b_a.astype(jnp.float32)))
    i = jax.nn.sigmoid(block_diag_linear(xr, w_x.astype(jnp.float32), b_x.astype(jnp.float32)))
    log_a = -LRU_C * r * jax.nn.softplus(-lam.astype(jnp.float32))
    a = jnp.exp(log_a)
    u = jnp.sqrt(-jnp.expm1(2.0 * log_a)) * (i * xr)

    def combine(c1, c2):
        a1, b1 = c1
        a2, b2 = c2
        return a1 * a2, a2 * b1 + b2

    _, h = lax.associative_scan(combine, (a, u), axis=1)
    return h * jax.nn.gelu(gb.astype(jnp.float32))


def hybrid_mixer(x, norm_g, w_in, conv_w, conv_b, w_a, b_a, w_x, b_x, lam,
                 attn_out_g, rec_out_g, w_out, cos, sin):
    B, S, _ = x.shape
    h = rms_norm(x, norm_g)
    proj = h @ w_in
    q, k, v, xb, gb = jnp.split(
        proj, [D_ATTN, 2 * D_ATTN, 3 * D_ATTN, 3 * D_ATTN + D_REC], axis=-1)
    q = apply_rope(q.reshape(B, S, N_ATTN_HEADS, HEAD_DIM), cos, sin)
    k = apply_rope(k.reshape(B, S, N_ATTN_HEADS, HEAD_DIM), cos, sin)
    v = v.reshape(B, S, N_ATTN_HEADS, HEAD_DIM).astype(jnp.float32)
    y_attn = dilated_attention_group(q, k, v).reshape(B, S, D_ATTN)
    y_rec = rglru_group(xb, gb, conv_w, conv_b, w_a, b_a, w_x, b_x, lam)
    merged = jnp.concatenate(
        [rms_norm(y_attn, attn_out_g), rms_norm(y_rec, rec_out_g)], axis=-1).astype(x.dtype)
    return merged @ w_out


def setup_inputs(seed: int = 0) -> dict:
    key = jax.random.key(seed)
    ks = jax.random.split(key, 20)
    L = DEPTH

    def nrm(k, shape, scale):
        return jax.random.normal(k, shape, jnp.float32) * scale

    def gain(k, shape):
        return 1.0 + 0.02 * jax.random.normal(k, shape, jnp.float32)

    a0 = jax.random.uniform(ks[12], (L, D_REC), jnp.float32, minval=0.9, maxval=0.999)
    return {
        "x": nrm(ks[0], (BATCH, SEQ, D_MODEL), 1.0),
        "ffn1_norm": gain(ks[1], (L, D_MODEL)),
        "ffn1_w_in": nrm(ks[2], (L, D_MODEL, 2 * D_FF), D_MODEL ** -0.5),
        "ffn1_w_out": nrm(ks[3], (L, D_FF, D_MODEL), D_FF ** -0.5),
        "mix_norm": gain(ks[4], (L, D_MODEL)),
        "w_in": nrm(ks[5], (L, D_MODEL, D_IN_PROJ), D_MODEL ** -0.5),
        "conv_w": nrm(ks[6], (L, CONV_WIDTH, D_REC), CONV_WIDTH ** -0.5),
        "conv_b": nrm(ks[7], (L, D_REC), 0.01),
        "rg_w_a": nrm(ks[8], (L, N_REC_BLOCKS, REC_BLOCK, REC_BLOCK), REC_BLOCK ** -0.5),
        "rg_b_a": nrm(ks[9], (L, D_REC), 0.01),
        "rg_w_x": nrm(ks[10], (L, N_REC_BLOCKS, REC_BLOCK, REC_BLOCK), REC_BLOCK ** -0.5),
        "rg_b_x": nrm(ks[11], (L, D_REC), 0.01),
        "rg_lambda": jnp.log(a0) - jnp.log1p(-a0),
        "attn_out_norm": gain(ks[13], (L, D_ATTN)),
        "rec_out_norm": gain(ks[14], (L, D_REC)),
        "w_out": nrm(ks[15], (L, D_MIX, D_MODEL), D_MIX ** -0.5),
        "ffn2_norm": gain(ks[16], (L, D_MODEL)),
        "ffn2_w_in": nrm(ks[17], (L, D_MODEL, 2 * D_FF), D_MODEL ** -0.5),
        "ffn2_w_out": nrm(ks[18], (L, D_FF, D_MODEL), D_FF ** -0.5),
        "final_norm": gain(ks[19], (D_MODEL,)),
    }


def reference(x, ffn1_norm, ffn1_w_in, ffn1_w_out, mix_norm, w_in, conv_w, conv_b,
              rg_w_a, rg_b_a, rg_w_x, rg_b_x, rg_lambda, attn_out_norm, rec_out_norm,
              w_out, ffn2_norm, ffn2_w_in, ffn2_w_out, final_norm):
    cos, sin = rope_tables(x.shape[1])
    for l in range(DEPTH):
        x = x + 0.5 * swiglu_ffn(x, ffn1_norm[l], ffn1_w_in[l], ffn1_w_out[l])
        x = x + hybrid_mixer(x, mix_norm[l], w_in[l], conv_w[l], conv_b[l],
                             rg_w_a[l], rg_b_a[l], rg_w_x[l], rg_b_x[l], rg_lambda[l],
                             attn_out_norm[l], rec_out_norm[l], w_out[l], cos, sin)
        x = x + 0.5 * swiglu_ffn(x, ffn2_norm[l], ffn2_w_in[l], ffn2_w_out[l])
    return rms_norm(x, final_norm)
```

```python
import functools
import math

import numpy as np
import jax
import jax.numpy as jnp
from jax import lax
from jax.experimental import pallas as pl
from jax.experimental.pallas import tpu as pltpu

HEAD_DIM = 64
HALF = HEAD_DIM // 2
N_HEADS = 8
D_ATTN = N_HEADS * HEAD_DIM
D_REC = 512
N_REC_BLOCKS = 8
REC_BLOCK = D_REC // N_REC_BLOCKS
CONV_WIDTH = 4
LRU_C = 8.0
ROPE_THETA = 10000.0
EPS = 1e-6
STEPS = 128
DILATIONS = (1, 4, 16)

LANES = 128
SUBLANES = 8
N_SLABS = D_ATTN // LANES
QUAD = 4
VMEM_LIMIT = 56 * 1024 * 1024

MXU_DTYPE = jnp.bfloat16
NEG_BIG = -1e30


def _mm(a, b):
    return jnp.dot(a.astype(MXU_DTYPE), b.astype(MXU_DTYPE), preferred_element_type=jnp.float32)


def _mm_nt(a, b):
    return lax.dot_general(a.astype(MXU_DTYPE), b.astype(MXU_DTYPE), (((1,), (1,)), ((), ())),
                           preferred_element_type=jnp.float32)


def _sigmoid(x):
    return 1.0 / (1.0 + jnp.exp(-x))


def _rms(x, g):
    var = jnp.mean(x * x, axis=-1, keepdims=True)
    return x * lax.rsqrt(var + EPS) * g


def _const_spec(shape):
    return pl.BlockSpec(shape, lambda *_: (0,) * len(shape), pipeline_mode=pl.Buffered(1))


def _ffn_body(x_ref, g_ref, win_ref, wout_ref, *rest, d_ff, chunk, final):
    if final:
        fg_ref, o_ref = rest
    else:
        (o_ref,) = rest
    x = x_ref[...]
    xn = _rms(x, g_ref[...]).astype(MXU_DTYPE)
    acc = None
    for lo in range(0, d_ff, chunk):
        gate = _mm(xn, win_ref[:, lo:lo + chunk])
        up = _mm(xn, win_ref[:, d_ff + lo:d_ff + lo + chunk])
        act = (gate * _sigmoid(gate) * up).astype(MXU_DTYPE)
        part = _mm(act, wout_ref[lo:lo + chunk, :])
        acc = part if acc is None else acc + part
    y = x + 0.5 * acc
    if final:
        y = _rms(y, fg_ref[...])
    o_ref[...] = y


def _ffn(x, g, w_in, w_out, final_g=None, *, tm=512, chunk=1408):
    t, d = x.shape
    d_ff = w_out.shape[0]
    final = final_g is not None
    in_specs = [pl.BlockSpec((tm, d), lambda i: (i, 0)),
                _const_spec((1, d)), _const_spec((d, 2 * d_ff)), _const_spec((d_ff, d))]
    args = [x, g.reshape(1, d), w_in, w_out]
    if final:
        in_specs.append(_const_spec((1, d)))
        args.append(final_g.reshape(1, d))
    return pl.pallas_call(
        functools.partial(_ffn_body, d_ff=d_ff, chunk=chunk, final=final),
        grid=(t // tm,),
        in_specs=in_specs,
        out_specs=pl.BlockSpec((tm, d), lambda i: (i, 0)),
        out_shape=jax.ShapeDtypeStruct((t, d), jnp.float32),
        compiler_params=pltpu.CompilerParams(dimension_semantics=("arbitrary",),
                                             vmem_limit_bytes=VMEM_LIMIT),
        name="ffn_final" if final else "ffn",
    )(*args)


N_PROJ_SLABS = 5 * N_SLABS


def _inproj_body(x_ref, g_ref, w_ref, cos_ref, sin_ref, o_ref):
    xn = _rms(x_ref[...], g_ref[...]).astype(MXU_DTYPE)
    proj = _mm(xn, w_ref[...])
    c = cos_ref[...]
    s = sin_ref[...]

    def slab(j):
        return proj[:, j * LANES:(j + 1) * LANES]

    for base, scale in ((0, 1.0 / math.sqrt(HEAD_DIM)), (N_SLABS, 1.0)):
        for g in range(N_SLABS // 2):
            t1 = slab(base + 2 * g)
            t2 = slab(base + 2 * g + 1)
            o_ref[0, base + 2 * g] = (t1 * c - t2 * s) * scale
            o_ref[0, base + 2 * g + 1] = (t2 * c + t1 * s) * scale
    for j in range(2 * N_SLABS, N_PROJ_SLABS):
        o_ref[0, j] = slab(j)


def _inproj(x, g, w, cos_t, sin_t, *, batch, seq, tm=512):
    t, d = x.shape
    n_seq = seq // tm
    return pl.pallas_call(
        _inproj_body,
        grid=(t // tm,),
        in_specs=[pl.BlockSpec((tm, d), lambda i: (i, 0)),
                  _const_spec((1, d)), _const_spec((d, N_PROJ_SLABS * LANES)),
                  pl.BlockSpec((tm, LANES), lambda i: (i % n_seq, 0)),
                  pl.BlockSpec((tm, LANES), lambda i: (i % n_seq, 0))],
        out_specs=pl.BlockSpec((1, N_PROJ_SLABS, tm, LANES), lambda i: (i // n_seq, 0, i % n_seq, 0)),
        out_shape=jax.ShapeDtypeStruct((batch, N_PROJ_SLABS, seq, LANES), jnp.float32),
        compiler_params=pltpu.CompilerParams(dimension_semantics=("arbitrary",),
                                             vmem_limit_bytes=VMEM_LIMIT),
        name="inproj",
    )(x, g.reshape(1, d), w, cos_t, sin_t)


def _load2(ref, start, n, stride):
    idx = pl.ds(start, n) if stride == 1 else pl.ds(start, n, stride=stride)
    return jnp.concatenate([ref[0, 0, idx, :], ref[0, 1, idx, :]], axis=1)


def _store2(ref, p, start, n, stride, val):
    idx = pl.ds(start, n) if stride == 1 else pl.ds(start, n, stride=stride)
    ref[p, 0, idx, :] = val[:, :LANES]
    ref[p, 1, idx, :] = val[:, LANES:]


def _attn_body(q_ref, k_ref, v_ref, o_ref, num_scr, m_scr, s_scr, *, seq):
    w2 = 2 * LANES
    lane = lax.broadcasted_iota(jnp.int32, (STEPS, w2), 1)
    q_head = (lane % LANES) // HALF
    v_head = lane // HEAD_DIM

    def bias(width, lo, hi):
        i = lax.broadcasted_iota(jnp.int32, (QUAD * STEPS, width), 0) % STEPS
        j = lax.broadcasted_iota(jnp.int32, (QUAD * STEPS, width), 1)
        d = j - i
        return jnp.where((d >= lo) & (d <= hi), 0.0, NEG_BIG).astype(jnp.float32)

    bias_first = bias(STEPS, -STEPS, 0)
    bias_band = bias(2 * STEPS, 0, STEPS)

    def unit(qb, kw, vw, b):
        qs = jnp.concatenate([jnp.where(q_head == h, qb, 0.0) for h in range(QUAD)], axis=0)
        sc = _mm_nt(qs, kw) + b
        m = jnp.max(sc, axis=1, keepdims=True)
        p = jnp.exp(sc - m)
        s = jnp.sum(p, axis=1, keepdims=True)
        ov = _mm(p, vw)

        def pick(full):
            out = full[0:STEPS]
            for h in range(1, QUAD):
                out = jnp.where(v_head == h, full[h * STEPS:(h + 1) * STEPS], out)
            return out

        return (pick(ov), pick(jnp.broadcast_to(m, (QUAD * STEPS, w2))),
                pick(jnp.broadcast_to(s, (QUAD * STEPS, w2))))

    def emit(p, start, stride, res):
        num, m, s = res
        _store2(num_scr, p, start, STEPS, stride, num)
        _store2(m_scr, p, start, STEPS, stride, m)
        _store2(s_scr, p, start, STEPS, stride, s)

    for p, dil in enumerate(DILATIONS):
        n_blocks = seq // (dil * STEPS)

        def klass(c, p=p, dil=dil, n_blocks=n_blocks):
            emit(p, c, dil, unit(_load2(q_ref, c, STEPS, dil), _load2(k_ref, c, STEPS, dil),
                                 _load2(v_ref, c, STEPS, dil), bias_first))

            def block(nb, carry):
                q0 = c + nb * (dil * STEPS)
                k0 = q0 - dil * STEPS
                if dil == 1:
                    q0 = pl.multiple_of(q0, STEPS)
                    k0 = pl.multiple_of(k0, STEPS)
                emit(p, q0, dil, unit(_load2(q_ref, q0, STEPS, dil), _load2(k_ref, k0, 2 * STEPS, dil),
                                      _load2(v_ref, k0, 2 * STEPS, dil), bias_band))
                return carry

            if n_blocks > 1:
                lax.fori_loop(1, n_blocks, block, 0)

        if dil == 1:
            klass(0)
        else:
            def klass_loop(c, carry):
                klass(c)
                return carry
            lax.fori_loop(0, dil, klass_loop, 0)

    def combine(i, carry):
        r0 = pl.multiple_of(i * STEPS, STEPS)
        rows = pl.ds(r0, STEPS)
        for sl in range(2):
            ms = [m_scr[p, sl, rows, :] for p in range(3)]
            m_all = jnp.maximum(jnp.maximum(ms[0], ms[1]), ms[2])
            ws = [jnp.exp(m_ - m_all) for m_ in ms]
            numer = (num_scr[0, sl, rows, :] * ws[0] + num_scr[1, sl, rows, :] * ws[1]
                     + num_scr[2, sl, rows, :] * ws[2])
            denom = (s_scr[0, sl, rows, :] * ws[0] + s_scr[1, sl, rows, :] * ws[1]
                     + s_scr[2, sl, rows, :] * ws[2])
            o_ref[0, sl, rows, :] = numer / denom
        return carry

    lax.fori_loop(0, seq // STEPS, combine, 0)


def _attention(proj, *, batch, seq):
    n_quads = N_HEADS // QUAD
    blk = (1, 2, seq, LANES)
    scr = pltpu.VMEM((len(DILATIONS), 2, seq, LANES), jnp.float32)
    return pl.pallas_call(
        functools.partial(_attn_body, seq=seq),
        grid=(batch, n_quads),
        in_specs=[pl.BlockSpec(blk, lambda b, g: (b, g, 0, 0)),
                  pl.BlockSpec(blk, lambda b, g: (b, n_quads + g, 0, 0)),
                  pl.BlockSpec(blk, lambda b, g: (b, 2 * n_quads + g, 0, 0))],
        out_specs=pl.BlockSpec(blk, lambda b, g: (b, g, 0, 0)),
        out_shape=jax.ShapeDtypeStruct((batch, N_SLABS, seq, LANES), jnp.float32),
        scratch_shapes=[scr, scr, scr],
        compiler_params=pltpu.CompilerParams(dimension_semantics=("arbitrary", "arbitrary"),
                                             vmem_limit_bytes=VMEM_LIMIT),
        name="dilated_attn",
    )(proj, proj, proj)


def _rglru_body(xb_ref, gb_ref, cw_ref, cb_ref, w_ref, ba_ref, bx_ref, lam_ref, o_ref,
                hl_scr, ac_scr, c_scr, *, seq):
    n_g = seq // SUBLANES
    row = lax.broadcasted_iota(jnp.int32, (n_g, LANES), 0)

    def shift_down(a):
        return jnp.where(row == 0, 0.0, pltpu.roll(a, 1, axis=0))

    for c in range(N_SLABS):
        xs = [xb_ref[0, c, pl.ds(r, n_g, stride=SUBLANES), :] for r in range(SUBLANES)]
        prev = [shift_down(xs[SUBLANES - k]) for k in range(1, CONV_WIDTH)]

        def stream(r):
            return xs[r] if r >= 0 else prev[-r - 1]

        cw = cw_ref[c]
        xr = []
        for r in range(SUBLANES):
            acc = cb_ref[c]
            for j in range(CONV_WIDTH):
                acc = acc + cw[j:j + 1, :] * stream(r - (CONV_WIDTH - 1) + j)
            xr.append(acc)
        y = _mm(jnp.concatenate(xr, axis=0), w_ref[c])
        z = -lam_ref[c]
        softplus = jnp.maximum(z, 0.0) + jnp.log(1.0 + jnp.exp(-jnp.abs(z)))
        hl = None
        ac = None
        for r in range(SUBLANES):
            yr = y[r * n_g:(r + 1) * n_g]
            r_gate = _sigmoid(yr[:, :LANES] + ba_ref[c])
            i_gate = _sigmoid(yr[:, LANES:] + bx_ref[c])
            log_a = -LRU_C * r_gate * softplus
            a = jnp.exp(log_a)
            u = jnp.sqrt(-jnp.tanh(log_a) * (a * a + 1.0)) * (i_gate * xr[r])
            hl = u if r == 0 else a * hl + u
            ac = a if r == 0 else a * ac
            hl_scr[c, r] = hl
            ac_scr[c, r] = ac
        c_scr[c, 0:SUBLANES, :] = jnp.zeros((SUBLANES, LANES), jnp.float32)

    last = SUBLANES - 1

    def carry_step(g, hs):
        new = []
        for c in range(N_SLABS):
            h = hl_scr[c, last, pl.ds(g, 1), :] + ac_scr[c, last, pl.ds(g, 1), :] * hs[c]
            c_scr[c, pl.ds(g + 1, 1), :] = h
            new.append(h)
        return tuple(new)

    zero = jnp.zeros((1, LANES), jnp.float32)
    lax.fori_loop(0, n_g, carry_step, (zero,) * N_SLABS, unroll=8)

    k0 = math.sqrt(2.0 / math.pi)
    for c in range(N_SLABS):
        h_in = c_scr[c, 0:n_g, :]
        for r in range(SUBLANES):
            h = hl_scr[c, r] + ac_scr[c, r] * h_in
            gb = gb_ref[0, c, pl.ds(r, n_g, stride=SUBLANES), :]
            gelu = 0.5 * gb * (1.0 + jnp.tanh(k0 * (gb + 0.044715 * (gb * gb * gb))))
            o_ref[0, c, pl.ds(r, n_g, stride=SUBLANES), :] = h * gelu


def _rglru(proj, cw, cb, w_pair, b_a, b_x, lam, *, batch, seq):
    blk = (1, N_SLABS, seq, LANES)
    n_g = seq // SUBLANES
    return pl.pallas_call(
        functools.partial(_rglru_body, seq=seq),
        grid=(batch,),
        in_specs=[pl.BlockSpec(blk, lambda b: (b, 3, 0, 0)),
                  pl.BlockSpec(blk, lambda b: (b, 4, 0, 0)),
                  _const_spec((N_SLABS, CONV_WIDTH, LANES)), _const_spec((N_SLABS, 1, LANES)),
                  _const_spec((N_SLABS, LANES, 2 * LANES)),
                  _const_spec((N_SLABS, 1, LANES)), _const_spec((N_SLABS, 1, LANES)),
                  _const_spec((N_SLABS, 1, LANES))],
        out_specs=pl.BlockSpec(blk, lambda b: (b, 0, 0, 0)),
        out_shape=jax.ShapeDtypeStruct((batch, N_SLABS, seq, LANES), jnp.float32),
        scratch_shapes=[pltpu.VMEM((N_SLABS, SUBLANES, n_g, LANES), jnp.float32),
                        pltpu.VMEM((N_SLABS, SUBLANES, n_g, LANES), jnp.float32),
                        pltpu.VMEM((N_SLABS, n_g + SUBLANES, LANES), jnp.float32)],
        compiler_params=pltpu.CompilerParams(dimension_semantics=("arbitrary",),
                                             vmem_limit_bytes=VMEM_LIMIT),
        name="rglru",
    )(proj, proj, cw, cb, w_pair, b_a, b_x, lam)


def _outproj_body(x_ref, ya_ref, yr_ref, ga_ref, gr_ref, w_ref, o_ref):
    ya = jnp.concatenate([ya_ref[0, j] for j in range(N_SLABS)], axis=1)
    yr = jnp.concatenate([yr_ref[0, j] for j in range(N_SLABS)], axis=1)
    merged = jnp.concatenate([_rms(ya, ga_ref[...]), _rms(yr, gr_ref[...])], axis=1)
    o_ref[...] = x_ref[...] + _mm(merged, w_ref[...])


def _outproj(x, y_attn, y_rec, g_attn, g_rec, w, *, seq, tm=512):
    t, d = x.shape
    n_seq = seq // tm
    yblk = pl.BlockSpec((1, N_SLABS, tm, LANES), lambda i: (i // n_seq, 0, i % n_seq, 0))
    return pl.pallas_call(
        _outproj_body,
        grid=(t // tm,),
        in_specs=[pl.BlockSpec((tm, d), lambda i: (i, 0)), yblk, yblk,
                  _const_spec((1, D_ATTN)), _const_spec((1, D_REC)), _const_spec((D_ATTN + D_REC, d))],
        out_specs=pl.BlockSpec((tm, d), lambda i: (i, 0)),
        out_shape=jax.ShapeDtypeStruct((t, d), jnp.float32),
        compiler_params=pltpu.CompilerParams(dimension_semantics=("arbitrary",),
                                             vmem_limit_bytes=VMEM_LIMIT),
        name="outproj",
    )(x, y_attn, y_rec, g_attn.reshape(1, -1), g_rec.reshape(1, -1), w)


def _qk_column_order():
    cols = []
    for g in range(N_HEADS // QUAD):
        for half in range(2):
            for h in range(QUAD):
                start = (QUAD * g + h) * HEAD_DIM + half * HALF
                cols.extend(range(start, start + HALF))
    return np.asarray(cols, dtype=np.int32)


def _rope_tables(seq):
    pos = jnp.arange(seq, dtype=jnp.float32)
    inv = ROPE_THETA ** (-jnp.arange(0, HEAD_DIM, 2, dtype=jnp.float32) / HEAD_DIM)
    ang = pos[:, None] * inv[None, :]
    return jnp.tile(jnp.cos(ang), (1, QUAD)), jnp.tile(jnp.sin(ang), (1, QUAD))


def _pair_block_diag(w):
    z = jnp.zeros((N_SLABS, REC_BLOCK, REC_BLOCK), w.dtype)
    top = jnp.concatenate([w[0::2], z], axis=2)
    bot = jnp.concatenate([z, w[1::2]], axis=2)
    return jnp.concatenate([top, bot], axis=1)


def kernel(x, ffn1_norm, ffn1_w_in, ffn1_w_out, mix_norm, w_in, conv_w, conv_b, rg_w_a, rg_b_a, rg_w_x,
           rg_b_x, rg_lambda, attn_out_norm, rec_out_norm, w_out, ffn2_norm, ffn2_w_in, ffn2_w_out,
           final_norm):
    batch, seq, d = x.shape
    depth = ffn1_norm.shape[0]
    cos_t, sin_t = _rope_tables(seq)
    order = _qk_column_order()
    h = x.reshape(batch * seq, d)
    for l in range(depth):
        wi = w_in[l]
        wi = jnp.concatenate([wi[:, order], wi[:, D_ATTN + order], wi[:, 2 * D_ATTN:]], axis=1)
        w_pair = jnp.concatenate([_pair_block_diag(rg_w_a[l]), _pair_block_diag(rg_w_x[l])], axis=2)

        def slabs(v):
            return v.reshape(N_SLABS, 1, LANES)

        h = _ffn(h, ffn1_norm[l], ffn1_w_in[l].astype(MXU_DTYPE), ffn1_w_out[l].astype(MXU_DTYPE))
        proj = _inproj(h, mix_norm[l], wi.astype(MXU_DTYPE), cos_t, sin_t, batch=batch, seq=seq)
        y_attn = _attention(proj, batch=batch, seq=seq)
        y_rec = _rglru(proj, conv_w[l].reshape(CONV_WIDTH, N_SLABS, LANES).transpose(1, 0, 2),
                       slabs(conv_b[l]), w_pair.astype(MXU_DTYPE), slabs(rg_b_a[l]), slabs(rg_b_x[l]),
                       slabs(rg_lambda[l]), batch=batch, seq=seq)
        h = _outproj(h, y_attn, y_rec, attn_out_norm[l], rec_out_norm[l], w_out[l].astype(MXU_DTYPE),
                     seq=seq)
        h = _ffn(h, ffn2_norm[l], ffn2_w_in[l].astype(MXU_DTYPE), ffn2_w_out[l].astype(MXU_DTYPE),
                 final_norm if l == depth - 1 else None)
    return h.reshape(batch, seq, d)
```

```python
import functools
import math

import numpy as np
import jax
import jax.numpy as jnp
from jax import lax
from jax.experimental import pallas as pl
from jax.experimental.pallas import tpu as pltpu

HEAD_DIM = 64
HALF = HEAD_DIM // 2
N_HEADS = 8
D_ATTN = N_HEADS * HEAD_DIM
D_REC = 512
N_REC_BLOCKS = 8
REC_BLOCK = D_REC // N_REC_BLOCKS
CONV_WIDTH = 4
LRU_C = 8.0
ROPE_THETA = 10000.0
EPS = 1e-6
STEPS = 128
DILATIONS = (1, 4, 16)

LANES = 128
SUBLANES = 8
N_SLABS = D_ATTN // LANES
MXU_TILE = 256
QUAD = 4
VMEM_LIMIT = 56 * 1024 * 1024

MXU_DTYPE = jnp.bfloat16
NEG_BIG = -1e30


def _mm(a, b):
    return jnp.dot(a.astype(MXU_DTYPE), b.astype(MXU_DTYPE), preferred_element_type=jnp.float32)


def _mm_nt(a, b):
    return lax.dot_general(a.astype(MXU_DTYPE), b.astype(MXU_DTYPE), (((1,), (1,)), ((), ())),
                           preferred_element_type=jnp.float32)


def _sigmoid(x):
    return 1.0 / (1.0 + jnp.exp(-x))


def _rms(x, g):
    var = jnp.mean(x * x, axis=-1, keepdims=True)
    return x * lax.rsqrt(var + EPS) * g


def _const_spec(shape):
    return pl.BlockSpec(shape, lambda *_: (0,) * len(shape), pipeline_mode=pl.Buffered(1))


def _ffn_chunks(d_ff, n_chunks):
    tiles = d_ff // MXU_TILE
    assert tiles * MXU_TILE == d_ff
    sizes = [(tiles // n_chunks + (1 if i < tiles % n_chunks else 0)) * MXU_TILE for i in range(n_chunks)]
    starts = [sum(sizes[:i]) for i in range(n_chunks)]
    return list(zip(starts, sizes))


def _ffn_body(x_ref, g_ref, win_ref, wout_ref, *rest, d_ff, chunks, final):
    if final:
        fg_ref, o_ref = rest
    else:
        (o_ref,) = rest
    x = x_ref[...]
    xn = _rms(x, g_ref[...]).astype(MXU_DTYPE)
    acc = None
    for lo, size in chunks:
        gate = _mm(xn, win_ref[:, lo:lo + size])
        up = _mm(xn, win_ref[:, d_ff + lo:d_ff + lo + size])
        act = (gate * _sigmoid(gate) * up).astype(MXU_DTYPE)
        part = _mm(act, wout_ref[lo:lo + size, :])
        acc = part if acc is None else acc + part
    y = x + 0.5 * acc
    if final:
        y = _rms(y, fg_ref[...])
    o_ref[...] = y


def _ffn(x, g, w_in, w_out, final_g=None, *, tm=1024, n_chunks=2):
    t, d = x.shape
    d_ff = w_out.shape[0]
    final = final_g is not None
    chunks = _ffn_chunks(d_ff, n_chunks)
    in_specs = [pl.BlockSpec((tm, d), lambda i: (i, 0)),
                _const_spec((1, d)), _const_spec((d, 2 * d_ff)), _const_spec((d_ff, d))]
    args = [x, g.reshape(1, d), w_in, w_out]
    if final:
        in_specs.append(_const_spec((1, d)))
        args.append(final_g.reshape(1, d))
    return pl.pallas_call(
        functools.partial(_ffn_body, d_ff=d_ff, chunks=chunks, final=final),
        grid=(t // tm,),
        in_specs=in_specs,
        out_specs=pl.BlockSpec((tm, d), lambda i: (i, 0)),
        out_shape=jax.ShapeDtypeStruct((t, d), jnp.float32),
        compiler_params=pltpu.CompilerParams(dimension_semantics=("arbitrary",),
                                             vmem_limit_bytes=VMEM_LIMIT),
        name="ffn_final" if final else "ffn",
    )(*args)


N_PROJ_SLABS = 5 * N_SLABS


def _inproj_body(x_ref, g_ref, w_ref, cos_ref, sin_ref, o_ref):
    xn = _rms(x_ref[...], g_ref[...]).astype(MXU_DTYPE)
    proj = _mm(xn, w_ref[...])
    c = cos_ref[...]
    s = sin_ref[...]

    def slab(j):
        return proj[:, j * LANES:(j + 1) * LANES]

    for base, scale in ((0, math.log2(math.e) / math.sqrt(HEAD_DIM)), (N_SLABS, 1.0)):
        for g in range(N_SLABS // 2):
            t1 = slab(base + 2 * g)
            t2 = slab(base + 2 * g + 1)
            o_ref[0, base + 2 * g] = (t1 * c - t2 * s) * scale
            o_ref[0, base + 2 * g + 1] = (t2 * c + t1 * s) * scale
    for j in range(2 * N_SLABS, N_PROJ_SLABS):
        o_ref[0, j] = slab(j)


def _inproj(x, g, w, cos_t, sin_t, *, batch, seq, tm=1024):
    t, d = x.shape
    n_seq = seq // tm
    return pl.pallas_call(
        _inproj_body,
        grid=(t // tm,),
        in_specs=[pl.BlockSpec((tm, d), lambda i: (i, 0)),
                  _const_spec((1, d)), _const_spec((d, N_PROJ_SLABS * LANES)),
                  pl.BlockSpec((tm, LANES), lambda i: (i % n_seq, 0)),
                  pl.BlockSpec((tm, LANES), lambda i: (i % n_seq, 0))],
        out_specs=pl.BlockSpec((1, N_PROJ_SLABS, tm, LANES), lambda i: (i // n_seq, 0, i % n_seq, 0)),
        out_shape=jax.ShapeDtypeStruct((batch, N_PROJ_SLABS, seq, LANES), jnp.float32),
        compiler_params=pltpu.CompilerParams(dimension_semantics=("arbitrary",),
                                             vmem_limit_bytes=VMEM_LIMIT),
        name="inproj",
    )(x, g.reshape(1, d), w, cos_t, sin_t)


def _rows(start, n, stride=1):
    return pl.ds(start, n) if stride == 1 else pl.ds(start, n, stride=stride)


def _load2(ref, lead, rows):
    return jnp.concatenate([ref[(lead, 0, rows, slice(None))], ref[(lead, 1, rows, slice(None))]], axis=1)


def _store2(ref, lead, rows, val):
    ref[(lead, 0, rows, slice(None))] = val[:, :LANES]
    ref[(lead, 1, rows, slice(None))] = val[:, LANES:]


MID_DIL = DILATIONS[1]
SUB_DIL = DILATIONS[2] // MID_DIL


CLS_SLOTS = 2


def _attn_body(q_ref, k_ref, v_ref, o_ref, bias_scr, nat_scr, cls_all, far_all, *, seq):
    w2 = 2 * LANES
    cls_len = seq // MID_DIL
    cls_blocks = cls_len // STEPS
    assert seq == DILATIONS[2] * STEPS and DILATIONS[0] == 1
    lane = lax.broadcasted_iota(jnp.int32, (SUBLANES, w2), 1)
    q_head = (lane % LANES) // HALF
    v_head = lane // HEAD_DIM
    groups = STEPS // SUBLANES

    def stack_heads(per_head):
        return jnp.concatenate([per_head(h, slice(a * SUBLANES, (a + 1) * SUBLANES))
                                for a in range(groups) for h in range(QUAD)], axis=0)

    def merge_heads(full):
        tiles = []
        for a in range(groups):
            base = a * QUAD * SUBLANES
            out = full[base:base + SUBLANES]
            for h in range(1, QUAD):
                out = jnp.where(v_head == h, full[base + h * SUBLANES:base + (h + 1) * SUBLANES], out)
            tiles.append(out)
        return jnp.concatenate(tiles, axis=0)

    @pl.when((pl.program_id(0) == 0) & (pl.program_id(1) == 0))
    def _():
        def table(width, fn):
            i = lax.broadcasted_iota(jnp.int32, (STEPS, width), 0)
            j = lax.broadcasted_iota(jnp.int32, (STEPS, width), 1)
            b = fn(i, j)
            return stack_heads(lambda h, rows: b[rows])

        def cls_bias(i, j):
            d = (cls_len - STEPS) + i - j
            in_mid = (d >= 0) & (d <= STEPS)
            in_far = (d >= 0) & (d % SUB_DIL == 0)
            return jnp.where(in_mid & in_far, 1.0, jnp.where(in_mid | in_far, 0.0, NEG_BIG))

        def band_bias(i, j):
            d = j - i
            return jnp.where((d >= 0) & (d <= STEPS), 0.0, NEG_BIG)

        bias_scr[:, 0:cls_len] = table(cls_len, cls_bias)
        bias_scr[:, cls_len:cls_len + 2 * STEPS] = table(2 * STEPS, band_bias)
        bias_scr[:, cls_len + 2 * STEPS:cls_len + 3 * STEPS] = table(STEPS, lambda i, j: jnp.where(j <= i, 0.0, NEG_BIG))

    def unit(qb, kw, vw, b):
        qs = stack_heads(lambda h, rows: jnp.where(q_head == h, qb[rows], 0.0))
        sc = _mm_nt(qs, kw) + b
        m = jnp.max(sc, axis=1, keepdims=True)
        p = jnp.exp2(sc - m)
        s = jnp.sum(p, axis=1, keepdims=True)
        ov = _mm(p, vw)
        return (merge_heads(ov), merge_heads(jnp.broadcast_to(m, (QUAD * STEPS, w2))),
                merge_heads(jnp.broadcast_to(s, (QUAD * STEPS, w2))))

    def emit(scr, rows, res):
        for kind in range(3):
            _store2(scr, kind, rows, res[kind])

    near_lo = cls_len
    emit(nat_scr, _rows(0, STEPS),
         unit(_load2(q_ref, 0, _rows(0, STEPS)), _load2(k_ref, 0, _rows(0, STEPS)),
              _load2(v_ref, 0, _rows(0, STEPS)), bias_scr[:, near_lo + 2 * STEPS:near_lo + 3 * STEPS]))

    def near_block(nb, carry):
        q0 = pl.multiple_of(nb * STEPS, STEPS)
        k_rows = _rows(pl.multiple_of(q0 - STEPS, STEPS), 2 * STEPS)
        emit(nat_scr, _rows(q0, STEPS),
             unit(_load2(q_ref, 0, _rows(q0, STEPS)), _load2(k_ref, 0, k_rows), _load2(v_ref, 0, k_rows),
                  bias_scr[:, near_lo:near_lo + 2 * STEPS]))
        return carry

    lax.fori_loop(1, seq // STEPS, near_block, 0, unroll=True)

    def klass(c, slot):
        cls_scr, far_scr = cls_all.at[slot], far_all.at[slot]
        for i, ref in enumerate((q_ref, k_ref, v_ref)):
            _store2(cls_scr, i, _rows(0, cls_len), _load2(ref, 0, _rows(c, cls_len, MID_DIL)))
        for nb in range(cls_blocks):
            k_rows = _rows(0, (nb + 1) * STEPS)
            emit(far_scr, _rows(nb * STEPS, STEPS),
                 unit(_load2(cls_scr, 0, _rows(nb * STEPS, STEPS)), _load2(cls_scr, 1, k_rows),
                      _load2(cls_scr, 2, k_rows), bias_scr[:, (cls_blocks - 1 - nb) * STEPS:cls_len]))
        for nb in range(cls_blocks):
            dense = _rows(nb * STEPS, STEPS)
            nat = _rows(c + nb * (MID_DIL * STEPS), STEPS, MID_DIL)
            for sl in range(2):
                m_near, m_far = nat_scr[1, sl, nat, :], far_scr[1, sl, dense, :]
                m_all = jnp.maximum(m_near, m_far)
                w_near, w_far = jnp.exp2(m_near - m_all), jnp.exp2(m_far - m_all)
                numer = nat_scr[0, sl, nat, :] * w_near + far_scr[0, sl, dense, :] * w_far
                denom = nat_scr[2, sl, nat, :] * w_near + far_scr[2, sl, dense, :] * w_far
                o_ref[0, sl, nat, :] = numer / denom

    def klass_group(i, carry):
        for slot in range(CLS_SLOTS):
            klass(i * CLS_SLOTS + slot, slot)
        return carry

    lax.fori_loop(0, MID_DIL // CLS_SLOTS, klass_group, 0)


def _attention(proj, *, batch, seq):
    n_quads = N_HEADS // QUAD
    blk = (1, 2, seq, LANES)
    nat = pltpu.VMEM((3, 2, seq, LANES), jnp.float32)
    cls = pltpu.VMEM((CLS_SLOTS, 3, 2, seq // MID_DIL, LANES), jnp.float32)
    return pl.pallas_call(
        functools.partial(_attn_body, seq=seq),
        grid=(batch, n_quads),
        in_specs=[pl.BlockSpec(blk, lambda b, g: (b, g, 0, 0)),
                  pl.BlockSpec(blk, lambda b, g: (b, n_quads + g, 0, 0)),
                  pl.BlockSpec(blk, lambda b, g: (b, 2 * n_quads + g, 0, 0))],
        out_specs=pl.BlockSpec(blk, lambda b, g: (b, g, 0, 0)),
        out_shape=jax.ShapeDtypeStruct((batch, N_SLABS, seq, LANES), jnp.float32),
        scratch_shapes=[pltpu.VMEM((QUAD * STEPS, seq // MID_DIL + 3 * STEPS), jnp.float32), nat, cls, cls],
        compiler_params=pltpu.CompilerParams(dimension_semantics=("arbitrary", "arbitrary"),
                                             vmem_limit_bytes=VMEM_LIMIT),
        name="dilated_attn",
    )(proj, proj, proj)


def _rglru_body(xb_ref, gb_ref, cw_ref, cb_ref, w_ref, ba_ref, bx_ref, lam_ref, o_ref,
                hl_scr, ac_scr, c_scr, *, seq):
    n_g = seq // SUBLANES
    row = lax.broadcasted_iota(jnp.int32, (n_g, LANES), 0)

    def shift_down(a):
        return jnp.where(row == 0, 0.0, pltpu.roll(a, 1, axis=0))

    for c in range(N_SLABS):
        xs = [xb_ref[0, c, pl.ds(r, n_g, stride=SUBLANES), :] for r in range(SUBLANES)]
        prev = [shift_down(xs[SUBLANES - k]) for k in range(1, CONV_WIDTH)]

        def stream(r):
            return xs[r] if r >= 0 else prev[-r - 1]

        cw = cw_ref[c]
        xr = []
        for r in range(SUBLANES):
            acc = cb_ref[c]
            for j in range(CONV_WIDTH):
                acc = acc + cw[j:j + 1, :] * stream(r - (CONV_WIDTH - 1) + j)
            xr.append(acc)
        y = _mm(jnp.concatenate(xr, axis=0), w_ref[c])
        z = -lam_ref[c]
        softplus = jnp.maximum(z, 0.0) + jnp.log(1.0 + jnp.exp(-jnp.abs(z)))
        hl = None
        ac = None
        for r in range(SUBLANES):
            yr = y[r * n_g:(r + 1) * n_g]
            r_gate = _sigmoid(yr[:, :LANES] + ba_ref[c])
            i_gate = _sigmoid(yr[:, LANES:] + bx_ref[c])
            log_a = -LRU_C * r_gate * softplus
            a = jnp.exp(log_a)
            u = jnp.sqrt(-jnp.tanh(log_a) * (a * a + 1.0)) * (i_gate * xr[r])
            hl = u if r == 0 else a * hl + u
            ac = a if r == 0 else a * ac
            hl_scr[c, r] = hl
            ac_scr[c, r] = ac
        c_scr[c, 0:SUBLANES, :] = jnp.zeros((SUBLANES, LANES), jnp.float32)

    last = SUBLANES - 1

    def carry_step(g, hs):
        new = []
        for c in range(N_SLABS):
            h = hl_scr[c, last, pl.ds(g, 1), :] + ac_scr[c, last, pl.ds(g, 1), :] * hs[c]
            c_scr[c, pl.ds(g + 1, 1), :] = h
            new.append(h)
        return tuple(new)

    zero = jnp.zeros((1, LANES), jnp.float32)
    lax.fori_loop(0, n_g, carry_step, (zero,) * N_SLABS, unroll=8)

    k0 = math.sqrt(2.0 / math.pi)
    for c in range(N_SLABS):
        h_in = c_scr[c, 0:n_g, :]
        for r in range(SUBLANES):
            h = hl_scr[c, r] + ac_scr[c, r] * h_in
            gb = gb_ref[0, c, pl.ds(r, n_g, stride=SUBLANES), :]
            gelu = 0.5 * gb * (1.0 + jnp.tanh(k0 * (gb + 0.044715 * (gb * gb * gb))))
            o_ref[0, c, pl.ds(r, n_g, stride=SUBLANES), :] = h * gelu


def _rglru(proj, cw, cb, w_pair, b_a, b_x, lam, *, batch, seq):
    blk = (1, N_SLABS, seq, LANES)
    n_g = seq // SUBLANES
    return pl.pallas_call(
        functools.partial(_rglru_body, seq=seq),
        grid=(batch,),
        in_specs=[pl.BlockSpec(blk, lambda b: (b, 3, 0, 0)),
                  pl.BlockSpec(blk, lambda b: (b, 4, 0, 0)),
                  _const_spec((N_SLABS, CONV_WIDTH, LANES)), _const_spec((N_SLABS, 1, LANES)),
                  _const_spec((N_SLABS, LANES, 2 * LANES)),
                  _const_spec((N_SLABS, 1, LANES)), _const_spec((N_SLABS, 1, LANES)),
                  _const_spec((N_SLABS, 1, LANES))],
        out_specs=pl.BlockSpec(blk, lambda b: (b, 0, 0, 0)),
        out_shape=jax.ShapeDtypeStruct((batch, N_SLABS, seq, LANES), jnp.float32),
        scratch_shapes=[pltpu.VMEM((N_SLABS, SUBLANES, n_g, LANES), jnp.float32),
                        pltpu.VMEM((N_SLABS, SUBLANES, n_g, LANES), jnp.float32),
                        pltpu.VMEM((N_SLABS, n_g + SUBLANES, LANES), jnp.float32)],
        compiler_params=pltpu.CompilerParams(dimension_semantics=("arbitrary",),
                                             vmem_limit_bytes=VMEM_LIMIT),
        name="rglru",
    )(proj, proj, cw, cb, w_pair, b_a, b_x, lam)


def _outproj_body(x_ref, ya_ref, yr_ref, ga_ref, gr_ref, w_ref, o_ref):
    ya = jnp.concatenate([ya_ref[0, j] for j in range(N_SLABS)], axis=1)
    yr = jnp.concatenate([yr_ref[0, j] for j in range(N_SLABS)], axis=1)
    merged = jnp.concatenate([_rms(ya, ga_ref[...]), _rms(yr, gr_ref[...])], axis=1)
    o_ref[...] = x_ref[...] + _mm(merged, w_ref[...])


def _outproj(x, y_attn, y_rec, g_attn, g_rec, w, *, seq, tm=1024):
    t, d = x.shape
    n_seq = seq // tm
    yblk = pl.BlockSpec((1, N_SLABS, tm, LANES), lambda i: (i // n_seq, 0, i % n_seq, 0))
    return pl.pallas_call(
        _outproj_body,
        grid=(t // tm,),
        in_specs=[pl.BlockSpec((tm, d), lambda i: (i, 0)), yblk, yblk,
                  _const_spec((1, D_ATTN)), _const_spec((1, D_REC)), _const_spec((D_ATTN + D_REC, d))],
        out_specs=pl.BlockSpec((tm, d), lambda i: (i, 0)),
        out_shape=jax.ShapeDtypeStruct((t, d), jnp.float32),
        compiler_params=pltpu.CompilerParams(dimension_semantics=("arbitrary",),
                                             vmem_limit_bytes=VMEM_LIMIT),
        name="outproj",
    )(x, y_attn, y_rec, g_attn.reshape(1, -1), g_rec.reshape(1, -1), w)


def _qk_column_order():
    cols = []
    for g in range(N_HEADS // QUAD):
        for half in range(2):
            for h in range(QUAD):
                start = (QUAD * g + h) * HEAD_DIM + half * HALF
                cols.extend(range(start, start + HALF))
    return np.asarray(cols, dtype=np.int32)


def _rope_tables(seq):
    pos = jnp.arange(seq, dtype=jnp.float32)
    inv = ROPE_THETA ** (-jnp.arange(0, HEAD_DIM, 2, dtype=jnp.float32) / HEAD_DIM)
    ang = pos[:, None] * inv[None, :]
    return jnp.tile(jnp.cos(ang), (1, QUAD)), jnp.tile(jnp.sin(ang), (1, QUAD))


def _pair_block_diag(w):
    z = jnp.zeros((N_SLABS, REC_BLOCK, REC_BLOCK), w.dtype)
    top = jnp.concatenate([w[0::2], z], axis=2)
    bot = jnp.concatenate([z, w[1::2]], axis=2)
    return jnp.concatenate([top, bot], axis=1)


def kernel(x, ffn1_norm, ffn1_w_in, ffn1_w_out, mix_norm, w_in, conv_w, conv_b, rg_w_a, rg_b_a, rg_w_x,
           rg_b_x, rg_lambda, attn_out_norm, rec_out_norm, w_out, ffn2_norm, ffn2_w_in, ffn2_w_out,
           final_norm):
    batch, seq, d = x.shape
    depth = ffn1_norm.shape[0]
    cos_t, sin_t = _rope_tables(seq)
    order = _qk_column_order()
    h = x.reshape(batch * seq, d)
    for l in range(depth):
        wi = w_in[l]
        wi = jnp.concatenate([wi[:, order], wi[:, D_ATTN + order], wi[:, 2 * D_ATTN:]], axis=1)
        w_pair = jnp.concatenate([_pair_block_diag(rg_w_a[l]), _pair_block_diag(rg_w_x[l])], axis=2)

        def slabs(v):
            return v.reshape(N_SLABS, 1, LANES)

        h = _ffn(h, ffn1_norm[l], ffn1_w_in[l].astype(MXU_DTYPE), ffn1_w_out[l].astype(MXU_DTYPE))
        proj = _inproj(h, mix_norm[l], wi.astype(MXU_DTYPE), cos_t, sin_t, batch=batch, seq=seq)
        y_attn = _attention(proj, batch=batch, seq=seq)
        y_rec = _rglru(proj, conv_w[l].reshape(CONV_WIDTH, N_SLABS, LANES).transpose(1, 0, 2),
                       slabs(conv_b[l]), w_pair.astype(MXU_DTYPE), slabs(rg_b_a[l]), slabs(rg_b_x[l]),
                       slabs(rg_lambda[l]), batch=batch, seq=seq)
        h = _outproj(h, y_attn, y_rec, attn_out_norm[l], rec_out_norm[l], w_out[l].astype(MXU_DTYPE),
                     seq=seq)
        h = _ffn(h, ffn2_norm[l], ffn2_w_in[l].astype(MXU_DTYPE), ffn2_w_out[l].astype(MXU_DTYPE),
                 final_norm if l == depth - 1 else None)
    return h.reshape(batch, seq, d)
```

```python
import functools
import math

import numpy as np
import jax
import jax.numpy as jnp
from jax import lax
from jax.experimental import pallas as pl
from jax.experimental.pallas import tpu as pltpu

HEAD_DIM = 64
HALF = HEAD_DIM // 2
N_HEADS = 8
D_ATTN = N_HEADS * HEAD_DIM
D_REC = 512
N_REC_BLOCKS = 8
REC_BLOCK = D_REC // N_REC_BLOCKS
CONV_WIDTH = 4
LRU_C = 8.0
ROPE_THETA = 10000.0
EPS = 1e-6
STEPS = 128
DILATIONS = (1, 4, 16)

LANES = 128
SUBLANES = 8
N_SLABS = D_ATTN // LANES
MXU_TILE = 256
QUAD = 4
VMEM_LIMIT = 56 * 1024 * 1024

MXU_DTYPE = jnp.bfloat16
NEG_BIG = -1e30


def _mm(a, b):
    return jnp.dot(a.astype(MXU_DTYPE), b.astype(MXU_DTYPE), preferred_element_type=jnp.float32)


def _mm_nt(a, b):
    return lax.dot_general(a.astype(MXU_DTYPE), b.astype(MXU_DTYPE), (((1,), (1,)), ((), ())),
                           preferred_element_type=jnp.float32)


def _sigmoid(x):
    return 1.0 / (1.0 + jnp.exp(-x))


def _rms(x, g):
    var = jnp.mean(x * x, axis=-1, keepdims=True)
    return x * lax.rsqrt(var + EPS) * g


def _const_spec(shape):
    return pl.BlockSpec(shape, lambda *_: (0,) * len(shape), pipeline_mode=pl.Buffered(1))


def _ffn_chunks(d_ff, n_chunks):
    tiles = d_ff // MXU_TILE
    assert tiles * MXU_TILE == d_ff
    sizes = [(tiles // n_chunks + (1 if i < tiles % n_chunks else 0)) * MXU_TILE for i in range(n_chunks)]
    starts = [sum(sizes[:i]) for i in range(n_chunks)]
    return list(zip(starts, sizes))


def _mixer_out(ya_ref, yr_ref, ga_ref, gr_ref, w_ref):
    ya = jnp.concatenate([ya_ref[0, j] for j in range(N_SLABS)], axis=1)
    yr = jnp.concatenate([yr_ref[0, j] for j in range(N_SLABS)], axis=1)
    merged = jnp.concatenate([_rms(ya, ga_ref[...]), _rms(yr, gr_ref[...])], axis=1)
    return _mm(merged, w_ref[...])


def _ffn_body(*refs, d_ff, chunks, mix, final):
    refs = list(refs)
    x = refs.pop(0)[...]
    if mix:
        x = x + _mixer_out(*refs[:5])
        del refs[:5]
    g_ref, win_ref, wout_ref = refs[:3]
    fg_ref = refs[3] if final else None
    o_ref = refs[-1]
    xn = _rms(x, g_ref[...]).astype(MXU_DTYPE)
    acc = None
    for lo, size in chunks:
        gate = _mm(xn, win_ref[:, lo:lo + size])
        up = _mm(xn, win_ref[:, d_ff + lo:d_ff + lo + size])
        act = (gate * _sigmoid(gate) * up).astype(MXU_DTYPE)
        part = _mm(act, wout_ref[lo:lo + size, :])
        acc = part if acc is None else acc + part
    y = x + 0.5 * acc
    if final:
        y = _rms(y, fg_ref[...])
    o_ref[...] = y


def _ffn(x, g, w_in, w_out, *, mix=None, seq=None, final_g=None, n_chunks=2):
    t, d = x.shape
    d_ff = w_out.shape[0]
    final = final_g is not None
    chunks = _ffn_chunks(d_ff, n_chunks)
    tm = 512 if mix is not None else 1024
    in_specs = [pl.BlockSpec((tm, d), lambda i: (i, 0))]
    args = [x]
    if mix is not None:
        y_attn, y_rec, g_attn, g_rec, w_mix = mix
        n_seq = seq // tm
        yblk = pl.BlockSpec((1, N_SLABS, tm, LANES), lambda i: (i // n_seq, 0, i % n_seq, 0))
        in_specs += [yblk, yblk, _const_spec((1, D_ATTN)), _const_spec((1, D_REC)),
                     _const_spec((D_ATTN + D_REC, d))]
        args += [y_attn, y_rec, g_attn.reshape(1, -1), g_rec.reshape(1, -1), w_mix]
    in_specs += [_const_spec((1, d)), _const_spec((d, 2 * d_ff)), _const_spec((d_ff, d))]
    args += [g.reshape(1, d), w_in, w_out]
    if final:
        in_specs.append(_const_spec((1, d)))
        args.append(final_g.reshape(1, d))
    return pl.pallas_call(
        functools.partial(_ffn_body, d_ff=d_ff, chunks=chunks, mix=mix is not None, final=final),
        grid=(t // tm,),
        in_specs=in_specs,
        out_specs=pl.BlockSpec((tm, d), lambda i: (i, 0)),
        out_shape=jax.ShapeDtypeStruct((t, d), jnp.float32),
        compiler_params=pltpu.CompilerParams(dimension_semantics=("arbitrary",),
                                             vmem_limit_bytes=VMEM_LIMIT),
        name=("mix_ffn" if mix is not None else "ffn") + ("_final" if final else ""),
    )(*args)


N_PROJ_SLABS = 5 * N_SLABS


def _inproj_body(x_ref, g_ref, w_ref, cos_ref, sin_ref, o_ref):
    xn = _rms(x_ref[...], g_ref[...]).astype(MXU_DTYPE)
    proj = _mm(xn, w_ref[...])
    c = cos_ref[...]
    s = sin_ref[...]

    def slab(j):
        return proj[:, j * LANES:(j + 1) * LANES]

    for base, scale in ((0, math.log2(math.e) / math.sqrt(HEAD_DIM)), (N_SLABS, 1.0)):
        for g in range(N_SLABS // 2):
            t1 = slab(base + 2 * g)
            t2 = slab(base + 2 * g + 1)
            o_ref[0, base + 2 * g] = (t1 * c - t2 * s) * scale
            o_ref[0, base + 2 * g + 1] = (t2 * c + t1 * s) * scale
    for j in range(2 * N_SLABS, N_PROJ_SLABS):
        o_ref[0, j] = slab(j)


def _inproj(x, g, w, cos_t, sin_t, *, batch, seq, tm=1024):
    t, d = x.shape
    n_seq = seq // tm
    return pl.pallas_call(
        _inproj_body,
        grid=(t // tm,),
        in_specs=[pl.BlockSpec((tm, d), lambda i: (i, 0)),
                  _const_spec((1, d)), _const_spec((d, N_PROJ_SLABS * LANES)),
                  pl.BlockSpec((tm, LANES), lambda i: (i % n_seq, 0)),
                  pl.BlockSpec((tm, LANES), lambda i: (i % n_seq, 0))],
        out_specs=pl.BlockSpec((1, N_PROJ_SLABS, tm, LANES), lambda i: (i // n_seq, 0, i % n_seq, 0)),
        out_shape=jax.ShapeDtypeStruct((batch, N_PROJ_SLABS, seq, LANES), jnp.float32),
        compiler_params=pltpu.CompilerParams(dimension_semantics=("arbitrary",),
                                             vmem_limit_bytes=VMEM_LIMIT),
        name="inproj",
    )(x, g.reshape(1, d), w, cos_t, sin_t)


def _rows(start, n, stride=1):
    return pl.ds(start, n) if stride == 1 else pl.ds(start, n, stride=stride)


def _load2(ref, lead, rows):
    return jnp.concatenate([ref[(lead, 0, rows, slice(None))], ref[(lead, 1, rows, slice(None))]], axis=1)


def _store2(ref, lead, rows, val):
    ref[(lead, 0, rows, slice(None))] = val[:, :LANES]
    ref[(lead, 1, rows, slice(None))] = val[:, LANES:]


MID_DIL = DILATIONS[1]
SUB_DIL = DILATIONS[2] // MID_DIL


CLS_SLOTS = 4


def _attn_body(q_ref, k_ref, v_ref, o_ref, bias_scr, nat_scr, cls_all, far_all, *, seq):
    w2 = 2 * LANES
    cls_len = seq // MID_DIL
    cls_blocks = cls_len // STEPS
    assert seq == DILATIONS[2] * STEPS and DILATIONS[0] == 1
    lane = lax.broadcasted_iota(jnp.int32, (SUBLANES, w2), 1)
    q_head = (lane % LANES) // HALF
    upper_head = lax.broadcasted_iota(jnp.int32, (SUBLANES, LANES), 1) >= HEAD_DIM
    groups = STEPS // SUBLANES

    def stack_heads(per_head):
        return jnp.concatenate([per_head(h, slice(a * SUBLANES, (a + 1) * SUBLANES))
                                for a in range(groups) for h in range(QUAD)], axis=0)

    def merge_heads(full):
        wide = full.shape[1] == w2
        tiles = []
        for a in range(groups):
            halves = []
            for t in range(2):
                lanes = slice(t * LANES, (t + 1) * LANES) if wide else slice(None)
                row0 = (a * QUAD + 2 * t) * SUBLANES
                halves.append(jnp.where(upper_head, full[row0 + SUBLANES:row0 + 2 * SUBLANES, lanes],
                                        full[row0:row0 + SUBLANES, lanes]))
            tiles.append(jnp.concatenate(halves, axis=1))
        return jnp.concatenate(tiles, axis=0)

    @pl.when((pl.program_id(0) == 0) & (pl.program_id(1) == 0))
    def _():
        def table(width, fn):
            i = lax.broadcasted_iota(jnp.int32, (STEPS, width), 0)
            j = lax.broadcasted_iota(jnp.int32, (STEPS, width), 1)
            b = fn(i, j)
            return stack_heads(lambda h, rows: b[rows])

        def cls_bias(i, j):
            d = (cls_len - STEPS) + i - j
            in_mid = (d >= 0) & (d <= STEPS)
            in_far = (d >= 0) & (d % SUB_DIL == 0)
            return jnp.where(in_mid & in_far, 1.0, jnp.where(in_mid | in_far, 0.0, NEG_BIG))

        def band_bias(i, j):
            d = j - i
            return jnp.where((d >= 0) & (d <= STEPS), 0.0, NEG_BIG)

        bias_scr[:, 0:cls_len] = table(cls_len, cls_bias)
        bias_scr[:, cls_len:cls_len + 2 * STEPS] = table(2 * STEPS, band_bias)
        bias_scr[:, cls_len + 2 * STEPS:cls_len + 3 * STEPS] = table(STEPS, lambda i, j: jnp.where(j <= i, 0.0, NEG_BIG))

    def unit(qb, kw, vw, b):
        qs = stack_heads(lambda h, rows: jnp.where(q_head == h, qb[rows], 0.0))
        sc = _mm_nt(qs, kw) + b
        m = jnp.max(sc, axis=1, keepdims=True)
        p = jnp.exp2(sc - m)
        s = jnp.sum(p, axis=1, keepdims=True)
        ov = _mm(p, vw)
        return (merge_heads(ov), merge_heads(jnp.broadcast_to(m, (QUAD * STEPS, LANES))),
                merge_heads(jnp.broadcast_to(s, (QUAD * STEPS, LANES))))

    def emit(scr, rows, res):
        for kind in range(3):
            _store2(scr, kind, rows, res[kind])

    near_lo = cls_len
    emit(nat_scr, _rows(0, STEPS),
         unit(_load2(q_ref, 0, _rows(0, STEPS)), _load2(k_ref, 0, _rows(0, STEPS)),
              _load2(v_ref, 0, _rows(0, STEPS)), bias_scr[:, near_lo + 2 * STEPS:near_lo + 3 * STEPS]))

    def near_block(nb, carry):
        q0 = pl.multiple_of(nb * STEPS, STEPS)
        k_rows = _rows(pl.multiple_of(q0 - STEPS, STEPS), 2 * STEPS)
        emit(nat_scr, _rows(q0, STEPS),
             unit(_load2(q_ref, 0, _rows(q0, STEPS)), _load2(k_ref, 0, k_rows), _load2(v_ref, 0, k_rows),
                  bias_scr[:, near_lo:near_lo + 2 * STEPS]))
        return carry

    lax.fori_loop(1, seq // STEPS, near_block, 0, unroll=True)

    def klass(c, slot):
        cls_scr, far_scr = cls_all.at[slot], far_all.at[slot]
        for i, ref in enumerate((q_ref, k_ref, v_ref)):
            _store2(cls_scr, i, _rows(0, cls_len), _load2(ref, 0, _rows(c, cls_len, MID_DIL)))
        for nb in range(cls_blocks):
            k_rows = _rows(0, (nb + 1) * STEPS)
            emit(far_scr, _rows(nb * STEPS, STEPS),
                 unit(_load2(cls_scr, 0, _rows(nb * STEPS, STEPS)), _load2(cls_scr, 1, k_rows),
                      _load2(cls_scr, 2, k_rows), bias_scr[:, (cls_blocks - 1 - nb) * STEPS:cls_len]))
        for nb in range(cls_blocks):
            dense = _rows(nb * STEPS, STEPS)
            nat = _rows(c + nb * (MID_DIL * STEPS), STEPS, MID_DIL)
            for sl in range(2):
                m_near, m_far = nat_scr[1, sl, nat, :], far_scr[1, sl, dense, :]
                m_all = jnp.maximum(m_near, m_far)
                w_near, w_far = jnp.exp2(m_near - m_all), jnp.exp2(m_far - m_all)
                numer = nat_scr[0, sl, nat, :] * w_near + far_scr[0, sl, dense, :] * w_far
                denom = nat_scr[2, sl, nat, :] * w_near + far_scr[2, sl, dense, :] * w_far
                o_ref[0, sl, nat, :] = numer / denom

    def klass_group(i, carry):
        for slot in range(CLS_SLOTS):
            klass(i * CLS_SLOTS + slot, slot)
        return carry

    lax.fori_loop(0, MID_DIL // CLS_SLOTS, klass_group, 0)


def _attention(proj, *, batch, seq):
    n_quads = N_HEADS // QUAD
    blk = (1, 2, seq, LANES)
    nat = pltpu.VMEM((3, 2, seq, LANES), jnp.float32)
    cls = pltpu.VMEM((CLS_SLOTS, 3, 2, seq // MID_DIL, LANES), jnp.float32)
    return pl.pallas_call(
        functools.partial(_attn_body, seq=seq),
        grid=(batch, n_quads),
        in_specs=[pl.BlockSpec(blk, lambda b, g: (b, g, 0, 0)),
                  pl.BlockSpec(blk, lambda b, g: (b, n_quads + g, 0, 0)),
                  pl.BlockSpec(blk, lambda b, g: (b, 2 * n_quads + g, 0, 0))],
        out_specs=pl.BlockSpec(blk, lambda b, g: (b, g, 0, 0)),
        out_shape=jax.ShapeDtypeStruct((batch, N_SLABS, seq, LANES), jnp.float32),
        scratch_shapes=[pltpu.VMEM((QUAD * STEPS, seq // MID_DIL + 3 * STEPS), jnp.float32), nat, cls, cls],
        compiler_params=pltpu.CompilerParams(dimension_semantics=("arbitrary", "arbitrary"),
                                             vmem_limit_bytes=VMEM_LIMIT),
        name="dilated_attn",
    )(proj, proj, proj)


def _rglru_body(xb_ref, gb_ref, cw_ref, cb_ref, w_ref, ba_ref, bx_ref, lam_ref, o_ref,
                hl_scr, ac_scr, c_scr, *, seq):
    n_g = seq // SUBLANES
    row = lax.broadcasted_iota(jnp.int32, (n_g, LANES), 0)

    def shift_down(a):
        return jnp.where(row == 0, 0.0, pltpu.roll(a, 1, axis=0))

    for c in range(N_SLABS):
        xs = [xb_ref[0, c, pl.ds(r, n_g, stride=SUBLANES), :] for r in range(SUBLANES)]
        prev = [shift_down(xs[SUBLANES - k]) for k in range(1, CONV_WIDTH)]

        def stream(r):
            return xs[r] if r >= 0 else prev[-r - 1]

        cw = cw_ref[c]
        xr = []
        for r in range(SUBLANES):
            acc = cb_ref[c]
            for j in range(CONV_WIDTH):
                acc = acc + cw[j:j + 1, :] * stream(r - (CONV_WIDTH - 1) + j)
            xr.append(acc)
        y = _mm(jnp.concatenate(xr, axis=0), w_ref[c])
        z = -lam_ref[c]
        decay = -LRU_C * (jnp.maximum(z, 0.0) + jnp.log(1.0 + jnp.exp(-jnp.abs(z))))
        hl = None
        ac = None
        for r in range(SUBLANES):
            yr = y[r * n_g:(r + 1) * n_g]
            r_gate = _sigmoid(yr[:, :LANES] + ba_ref[c])
            i_gate = _sigmoid(yr[:, LANES:] + bx_ref[c])
            log_a = r_gate * decay
            a = jnp.exp(log_a)
            u = jnp.sqrt(-jnp.tanh(log_a) * (a * a + 1.0)) * (i_gate * xr[r])
            hl = u if r == 0 else a * hl + u
            ac = a if r == 0 else a * ac
            hl_scr[c, r] = hl
            ac_scr[c, r] = ac
        c_scr[c, 0:SUBLANES, :] = jnp.zeros((SUBLANES, LANES), jnp.float32)

    last = SUBLANES - 1

    def carry_step(g, hs):
        new = []
        for c in range(N_SLABS):
            h = hl_scr[c, last, pl.ds(g, 1), :] + ac_scr[c, last, pl.ds(g, 1), :] * hs[c]
            c_scr[c, pl.ds(g + 1, 1), :] = h
            new.append(h)
        return tuple(new)

    zero = jnp.zeros((1, LANES), jnp.float32)
    lax.fori_loop(0, n_g, carry_step, (zero,) * N_SLABS, unroll=8)

    k0 = math.sqrt(2.0 / math.pi)
    for c in range(N_SLABS):
        h_in = c_scr[c, 0:n_g, :]
        for r in range(SUBLANES):
            h = hl_scr[c, r] + ac_scr[c, r] * h_in
            gb = gb_ref[0, c, pl.ds(r, n_g, stride=SUBLANES), :]
            gelu = 0.5 * gb * (1.0 + jnp.tanh(k0 * (gb + 0.044715 * (gb * gb * gb))))
            o_ref[0, c, pl.ds(r, n_g, stride=SUBLANES), :] = h * gelu


def _rglru(proj, cw, cb, w_pair, b_a, b_x, lam, *, batch, seq):
    blk = (1, N_SLABS, seq, LANES)
    n_g = seq // SUBLANES
    return pl.pallas_call(
        functools.partial(_rglru_body, seq=seq),
        grid=(batch,),
        in_specs=[pl.BlockSpec(blk, lambda b: (b, 3, 0, 0)),
                  pl.BlockSpec(blk, lambda b: (b, 4, 0, 0)),
                  _const_spec((N_SLABS, CONV_WIDTH, LANES)), _const_spec((N_SLABS, 1, LANES)),
                  _const_spec((N_SLABS, LANES, 2 * LANES)),
                  _const_spec((N_SLABS, 1, LANES)), _const_spec((N_SLABS, 1, LANES)),
                  _const_spec((N_SLABS, 1, LANES))],
        out_specs=pl.BlockSpec(blk, lambda b: (b, 0, 0, 0)),
        out_shape=jax.ShapeDtypeStruct((batch, N_SLABS, seq, LANES), jnp.float32),
        scratch_shapes=[pltpu.VMEM((N_SLABS, SUBLANES, n_g, LANES), jnp.float32),
                        pltpu.VMEM((N_SLABS, SUBLANES, n_g, LANES), jnp.float32),
                        pltpu.VMEM((N_SLABS, n_g + SUBLANES, LANES), jnp.float32)],
        compiler_params=pltpu.CompilerParams(dimension_semantics=("arbitrary",),
                                             vmem_limit_bytes=VMEM_LIMIT),
        name="rglru",
    )(proj, proj, cw, cb, w_pair, b_a, b_x, lam)


def _rope_halves_apart(w):
    d = w.shape[0]
    w = w.reshape(d, N_HEADS // QUAD, QUAD, 2, HALF)
    return w.transpose(0, 1, 3, 2, 4).reshape(d, D_ATTN)


def _rope_tables(seq):
    pos = jnp.arange(seq, dtype=jnp.float32)
    inv = ROPE_THETA ** (-jnp.arange(0, HEAD_DIM, 2, dtype=jnp.float32) / HEAD_DIM)
    ang = pos[:, None] * inv[None, :]
    return jnp.tile(jnp.cos(ang), (1, QUAD)), jnp.tile(jnp.sin(ang), (1, QUAD))


def _pair_block_diag(w):
    z = jnp.zeros((N_SLABS, REC_BLOCK, REC_BLOCK), w.dtype)
    top = jnp.concatenate([w[0::2], z], axis=2)
    bot = jnp.concatenate([z, w[1::2]], axis=2)
    return jnp.concatenate([top, bot], axis=1)


def kernel(x, ffn1_norm, ffn1_w_in, ffn1_w_out, mix_norm, w_in, conv_w, conv_b, rg_w_a, rg_b_a, rg_w_x,
           rg_b_x, rg_lambda, attn_out_norm, rec_out_norm, w_out, ffn2_norm, ffn2_w_in, ffn2_w_out,
           final_norm):
    batch, seq, d = x.shape
    depth = ffn1_norm.shape[0]
    cos_t, sin_t = _rope_tables(seq)
    h = x.reshape(batch * seq, d)
    for l in range(depth):
        wi = w_in[l].astype(MXU_DTYPE)
        wi = jnp.concatenate([_rope_halves_apart(wi[:, :D_ATTN]), _rope_halves_apart(wi[:, D_ATTN:2 * D_ATTN]),
                              wi[:, 2 * D_ATTN:]], axis=1)
        w_pair = jnp.concatenate([_pair_block_diag(rg_w_a[l]), _pair_block_diag(rg_w_x[l])], axis=2)

        def slabs(v):
            return v.reshape(N_SLABS, 1, LANES)

        h = _ffn(h, ffn1_norm[l], ffn1_w_in[l].astype(MXU_DTYPE), ffn1_w_out[l].astype(MXU_DTYPE))
        proj = _inproj(h, mix_norm[l], wi.astype(MXU_DTYPE), cos_t, sin_t, batch=batch, seq=seq)
        y_attn = _attention(proj, batch=batch, seq=seq)
        y_rec = _rglru(proj, conv_w[l].reshape(CONV_WIDTH, N_SLABS, LANES).transpose(1, 0, 2),
                       slabs(conv_b[l]), w_pair.astype(MXU_DTYPE), slabs(rg_b_a[l]), slabs(rg_b_x[l]),
                       slabs(rg_lambda[l]), batch=batch, seq=seq)
        h = _ffn(h, ffn2_norm[l], ffn2_w_in[l].astype(MXU_DTYPE), ffn2_w_out[l].astype(MXU_DTYPE),
                 mix=(y_attn, y_rec, attn_out_norm[l], rec_out_norm[l], w_out[l].astype(MXU_DTYPE)),
                 seq=seq, final_g=final_norm if l == depth - 1 else None)
    return h.reshape(batch, seq, d)
```

```python
import functools
import math

import jax
import jax.numpy as jnp
from jax import lax
from jax.experimental import pallas as pl
from jax.experimental.pallas import tpu as pltpu

HEAD_DIM = 64
HALF = HEAD_DIM // 2
N_HEADS = 8
D_ATTN = N_HEADS * HEAD_DIM
D_REC = 512
N_REC_BLOCKS = 8
REC_BLOCK = D_REC // N_REC_BLOCKS
CONV_WIDTH = 4
LRU_C = 8.0
ROPE_THETA = 10000.0
EPS = 1e-6
STEPS = 128
DILATIONS = (1, 4, 16)

LANES = 128
SUBLANES = 8
N_SLABS = D_ATTN // LANES
MXU_TILE = 256
QUAD = 4
VMEM_LIMIT = 56 * 1024 * 1024

MXU_DTYPE = jnp.bfloat16
NEG_BIG = -1e30


def _mm(a, b):
    return jnp.dot(a.astype(MXU_DTYPE), b.astype(MXU_DTYPE), preferred_element_type=jnp.float32)


def _mm_nt(a, b):
    return lax.dot_general(a.astype(MXU_DTYPE), b.astype(MXU_DTYPE), (((1,), (1,)), ((), ())),
                           preferred_element_type=jnp.float32)


def _sigmoid(x):
    return 1.0 / (1.0 + jnp.exp(-x))


def _rms(x, g):
    var = jnp.mean(x * x, axis=-1, keepdims=True)
    return x * lax.rsqrt(var + EPS) * g


def _const_spec(shape):
    return pl.BlockSpec(shape, lambda *_: (0,) * len(shape), pipeline_mode=pl.Buffered(1))


def _ffn_chunks(d_ff, n_chunks):
    tiles = d_ff // MXU_TILE
    assert tiles * MXU_TILE == d_ff
    sizes = [(tiles // n_chunks + (1 if i < tiles % n_chunks else 0)) * MXU_TILE for i in range(n_chunks)]
    starts = [sum(sizes[:i]) for i in range(n_chunks)]
    return list(zip(starts, sizes))


def _mixer_out(ya_ref, yr_ref, ga_ref, gr_ref, w_ref):
    ya = jnp.concatenate([ya_ref[0, j] for j in range(N_SLABS)], axis=1)
    yr = jnp.concatenate([yr_ref[0, j] for j in range(N_SLABS)], axis=1)
    merged = jnp.concatenate([_rms(ya, ga_ref[...]), _rms(yr, gr_ref[...])], axis=1)
    return _mm(merged, w_ref[...])


def _ffn_body(*refs, d_ff, chunks, mix, final):
    refs = list(refs)
    x = refs.pop(0)[...]
    if mix:
        x = x + _mixer_out(*refs[:5])
        del refs[:5]
    g_ref, win_ref, wout_ref = refs[:3]
    fg_ref = refs[3] if final else None
    o_ref = refs[-1]
    xn = _rms(x, g_ref[...]).astype(MXU_DTYPE)
    acc = None
    for lo, size in chunks:
        gate = _mm(xn, win_ref[:, lo:lo + size])
        up = _mm(xn, win_ref[:, d_ff + lo:d_ff + lo + size])
        act = (gate * _sigmoid(gate) * up).astype(MXU_DTYPE)
        part = _mm(act, wout_ref[lo:lo + size, :])
        acc = part if acc is None else acc + part
    y = x + 0.5 * acc
    if final:
        y = _rms(y, fg_ref[...])
    o_ref[...] = y


def _ffn(x, g, w_in, w_out, *, mix=None, seq=None, final_g=None, n_chunks=2):
    t, d = x.shape
    d_ff = w_out.shape[0]
    final = final_g is not None
    chunks = _ffn_chunks(d_ff, n_chunks)
    tm = 512 if mix is not None else 1024
    in_specs = [pl.BlockSpec((tm, d), lambda i: (i, 0))]
    args = [x]
    if mix is not None:
        y_attn, y_rec, g_attn, g_rec, w_mix = mix
        n_seq = seq // tm
        yblk = pl.BlockSpec((1, N_SLABS, tm, LANES), lambda i: (i // n_seq, 0, i % n_seq, 0))
        in_specs += [yblk, yblk, _const_spec((1, D_ATTN)), _const_spec((1, D_REC)),
                     _const_spec((D_ATTN + D_REC, d))]
        args += [y_attn, y_rec, g_attn.reshape(1, -1), g_rec.reshape(1, -1), w_mix]
    in_specs += [_const_spec((1, d)), _const_spec((d, 2 * d_ff)), _const_spec((d_ff, d))]
    args += [g.reshape(1, d), w_in, w_out]
    if final:
        in_specs.append(_const_spec((1, d)))
        args.append(final_g.reshape(1, d))
    return pl.pallas_call(
        functools.partial(_ffn_body, d_ff=d_ff, chunks=chunks, mix=mix is not None, final=final),
        grid=(t // tm,),
        in_specs=in_specs,
        out_specs=pl.BlockSpec((tm, d), lambda i: (i, 0)),
        out_shape=jax.ShapeDtypeStruct((t, d), jnp.float32),
        compiler_params=pltpu.CompilerParams(dimension_semantics=("arbitrary",),
                                             vmem_limit_bytes=VMEM_LIMIT),
        name=("mix_ffn" if mix is not None else "ffn") + ("_final" if final else ""),
    )(*args)


N_PROJ_SLABS = 5 * N_SLABS


def _inproj_body(x_ref, g_ref, w_ref, cos_ref, sin_ref, o_ref):
    xn = _rms(x_ref[...], g_ref[...]).astype(MXU_DTYPE)
    proj = _mm(xn, w_ref[...])
    c = cos_ref[...]
    s = sin_ref[...]

    def slab(j):
        return proj[:, j * LANES:(j + 1) * LANES]

    for base, scale in ((0, math.log2(math.e) / math.sqrt(HEAD_DIM)), (N_SLABS, 1.0)):
        for g in range(N_SLABS // 2):
            t1 = slab(base + 2 * g)
            t2 = slab(base + 2 * g + 1)
            o_ref[0, base + 2 * g] = (t1 * c - t2 * s) * scale
            o_ref[0, base + 2 * g + 1] = (t2 * c + t1 * s) * scale
    for j in range(2 * N_SLABS, N_PROJ_SLABS):
        o_ref[0, j] = slab(j)


def _inproj(x, g, w, cos_t, sin_t, *, batch, seq, tm=1024):
    t, d = x.shape
    n_seq = seq // tm
    return pl.pallas_call(
        _inproj_body,
        grid=(t // tm,),
        in_specs=[pl.BlockSpec((tm, d), lambda i: (i, 0)),
                  _const_spec((1, d)), _const_spec((d, N_PROJ_SLABS * LANES)),
                  pl.BlockSpec((tm, LANES), lambda i: (i % n_seq, 0)),
                  pl.BlockSpec((tm, LANES), lambda i: (i % n_seq, 0))],
        out_specs=pl.BlockSpec((1, N_PROJ_SLABS, tm, LANES), lambda i: (i // n_seq, 0, i % n_seq, 0)),
        out_shape=jax.ShapeDtypeStruct((batch, N_PROJ_SLABS, seq, LANES), jnp.float32),
        compiler_params=pltpu.CompilerParams(dimension_semantics=("arbitrary",),
                                             vmem_limit_bytes=VMEM_LIMIT),
        name="inproj",
    )(x, g.reshape(1, d), w, cos_t, sin_t)


def _rows(start, n, stride=1):
    return pl.ds(start, n) if stride == 1 else pl.ds(start, n, stride=stride)


def _load2(ref, lead, rows):
    return jnp.concatenate([ref[(lead, 0, rows, slice(None))], ref[(lead, 1, rows, slice(None))]], axis=1)


def _store2(ref, lead, rows, val):
    ref[(lead, 0, rows, slice(None))] = val[:, :LANES]
    ref[(lead, 1, rows, slice(None))] = val[:, LANES:]


MID_DIL = DILATIONS[1]
SUB_DIL = DILATIONS[2] // MID_DIL
CLS_SLOTS = 4


def _attn_body(q_ref, k_ref, v_ref, *rest, seq, n_casts):
    cast_in, o_ref, cast_out = rest[:n_casts], rest[n_casts], rest[n_casts + 1:2 * n_casts + 1]
    bias_scr, nat_scr, cls_all, far_all = rest[2 * n_casts + 1:]
    for src, dst in zip(cast_in, cast_out):
        dst[...] = src[0].astype(dst.dtype)
    w2 = 2 * LANES
    cls_len = seq // MID_DIL
    cls_blocks = cls_len // STEPS
    assert seq == DILATIONS[2] * STEPS and DILATIONS[0] == 1
    lane = lax.broadcasted_iota(jnp.int32, (SUBLANES, w2), 1)
    q_head = (lane % LANES) // HALF
    upper_head = lax.broadcasted_iota(jnp.int32, (SUBLANES, LANES), 1) >= HEAD_DIM
    groups = STEPS // SUBLANES

    def stack_heads(per_head):
        return jnp.concatenate([per_head(h, slice(a * SUBLANES, (a + 1) * SUBLANES))
                                for a in range(groups) for h in range(QUAD)], axis=0)

    def merge_heads(full):
        wide = full.shape[1] == w2
        tiles = []
        for a in range(groups):
            halves = []
            for t in range(2):
                lanes = slice(t * LANES, (t + 1) * LANES) if wide else slice(None)
                row0 = (a * QUAD + 2 * t) * SUBLANES
                halves.append(jnp.where(upper_head, full[row0 + SUBLANES:row0 + 2 * SUBLANES, lanes],
                                        full[row0:row0 + SUBLANES, lanes]))
            tiles.append(jnp.concatenate(halves, axis=1))
        return jnp.concatenate(tiles, axis=0)

    @pl.when((pl.program_id(0) == 0) & (pl.program_id(1) == 0))
    def _():
        def table(width, fn):
            i = lax.broadcasted_iota(jnp.int32, (STEPS, width), 0)
            j = lax.broadcasted_iota(jnp.int32, (STEPS, width), 1)
            b = fn(i, j)
            return stack_heads(lambda h, rows: b[rows])

        def cls_bias(i, j):
            d = (cls_len - STEPS) + i - j
            in_mid = (d >= 0) & (d <= STEPS)
            in_far = (d >= 0) & (d % SUB_DIL == 0)
            return jnp.where(in_mid & in_far, 1.0, jnp.where(in_mid | in_far, 0.0, NEG_BIG))

        def band_bias(i, j):
            d = j - i
            return jnp.where((d >= 0) & (d <= STEPS), 0.0, NEG_BIG)

        bias_scr[:, 0:cls_len] = table(cls_len, cls_bias)
        bias_scr[:, cls_len:cls_len + 2 * STEPS] = table(2 * STEPS, band_bias)
        bias_scr[:, cls_len + 2 * STEPS:cls_len + 3 * STEPS] = table(STEPS, lambda i, j: jnp.where(j <= i, 0.0, NEG_BIG))

    def unit(qb, kw, vw, b):
        qs = stack_heads(lambda h, rows: jnp.where(q_head == h, qb[rows], 0.0))
        sc = _mm_nt(qs, kw) + b
        m = jnp.max(sc, axis=1, keepdims=True)
        p = jnp.exp2(sc - m)
        s = jnp.sum(p, axis=1, keepdims=True)
        ov = _mm(p, vw)
        return (merge_heads(ov), merge_heads(jnp.broadcast_to(m, (QUAD * STEPS, LANES))),
                merge_heads(jnp.broadcast_to(s, (QUAD * STEPS, LANES))))

    def emit(scr, rows, res):
        for kind in range(3):
            _store2(scr, kind, rows, res[kind])

    near_lo = cls_len
    emit(nat_scr, _rows(0, STEPS),
         unit(_load2(q_ref, 0, _rows(0, STEPS)), _load2(k_ref, 0, _rows(0, STEPS)),
              _load2(v_ref, 0, _rows(0, STEPS)), bias_scr[:, near_lo + 2 * STEPS:near_lo + 3 * STEPS]))

    def near_block(nb, carry):
        q0 = pl.multiple_of(nb * STEPS, STEPS)
        k_rows = _rows(pl.multiple_of(q0 - STEPS, STEPS), 2 * STEPS)
        emit(nat_scr, _rows(q0, STEPS),
             unit(_load2(q_ref, 0, _rows(q0, STEPS)), _load2(k_ref, 0, k_rows), _load2(v_ref, 0, k_rows),
                  bias_scr[:, near_lo:near_lo + 2 * STEPS]))
        return carry

    lax.fori_loop(1, seq // STEPS, near_block, 0, unroll=True)

    def klass(c, slot):
        cls_scr, far_scr = cls_all.at[slot], far_all.at[slot]
        for i, ref in enumerate((q_ref, k_ref, v_ref)):
            _store2(cls_scr, i, _rows(0, cls_len), _load2(ref, 0, _rows(c, cls_len, MID_DIL)))
        for nb in range(cls_blocks):
            k_rows = _rows(0, (nb + 1) * STEPS)
            emit(far_scr, _rows(nb * STEPS, STEPS),
                 unit(_load2(cls_scr, 0, _rows(nb * STEPS, STEPS)), _load2(cls_scr, 1, k_rows),
                      _load2(cls_scr, 2, k_rows), bias_scr[:, (cls_blocks - 1 - nb) * STEPS:cls_len]))
        for nb in range(cls_blocks):
            dense = _rows(nb * STEPS, STEPS)
            nat = _rows(c + nb * (MID_DIL * STEPS), STEPS, MID_DIL)
            for sl in range(2):
                m_near, m_far = nat_scr[1, sl, nat, :], far_scr[1, sl, dense, :]
                m_all = jnp.maximum(m_near, m_far)
                w_near, w_far = jnp.exp2(m_near - m_all), jnp.exp2(m_far - m_all)
                numer = nat_scr[0, sl, nat, :] * w_near + far_scr[0, sl, dense, :] * w_far
                denom = nat_scr[2, sl, nat, :] * w_near + far_scr[2, sl, dense, :] * w_far
                o_ref[0, sl, nat, :] = numer / denom

    def klass_group(i, carry):
        for slot in range(CLS_SLOTS):
            klass(i * CLS_SLOTS + slot, slot)
        return carry

    lax.fori_loop(0, MID_DIL // CLS_SLOTS, klass_group, 0)


def _attention(proj, casts, *, batch, seq):
    n_quads = N_HEADS // QUAD
    n_steps = batch * n_quads
    blk = (1, 2, seq, LANES)
    nat = pltpu.VMEM((3, 2, seq, LANES), jnp.float32)
    cls = pltpu.VMEM((CLS_SLOTS, 3, 2, seq // MID_DIL, LANES), jnp.float32)
    cast_in_specs, cast_out_specs, cast_out_shapes = [], [], []
    for w, layer in casts:
        _, rows, cols = w.shape
        blk_rows = rows // n_steps
        assert blk_rows * n_steps == rows and blk_rows % (2 * SUBLANES) == 0, w.shape
        cast_in_specs.append(pl.BlockSpec((1, blk_rows, cols), lambda b, g, layer=layer: (layer, b * n_quads + g, 0)))
        cast_out_specs.append(pl.BlockSpec((blk_rows, cols), lambda b, g: (b * n_quads + g, 0)))
        cast_out_shapes.append(jax.ShapeDtypeStruct((rows, cols), MXU_DTYPE))
    outs = pl.pallas_call(
        functools.partial(_attn_body, seq=seq, n_casts=len(casts)),
        grid=(batch, n_quads),
        in_specs=[pl.BlockSpec(blk, lambda b, g: (b, g, 0, 0)),
                  pl.BlockSpec(blk, lambda b, g: (b, n_quads + g, 0, 0)),
                  pl.BlockSpec(blk, lambda b, g: (b, 2 * n_quads + g, 0, 0))] + cast_in_specs,
        out_specs=[pl.BlockSpec(blk, lambda b, g: (b, g, 0, 0))] + cast_out_specs,
        out_shape=[jax.ShapeDtypeStruct((batch, N_SLABS, seq, LANES), jnp.float32)] + cast_out_shapes,
        scratch_shapes=[pltpu.VMEM((QUAD * STEPS, seq // MID_DIL + 3 * STEPS), jnp.float32), nat, cls, cls],
        compiler_params=pltpu.CompilerParams(dimension_semantics=("arbitrary", "arbitrary"),
                                             vmem_limit_bytes=VMEM_LIMIT),
        name="dilated_attn",
    )(proj, proj, proj, *[w for w, _ in casts])
    return outs[0], outs[1:]


def _rglru_body(xb_ref, gb_ref, cw_ref, cb_ref, w_ref, ba_ref, bx_ref, lam_ref, o_ref,
                hl_scr, ac_scr, c_scr, *, seq):
    n_g = seq // SUBLANES
    row = lax.broadcasted_iota(jnp.int32, (n_g, LANES), 0)

    def shift_down(a):
        return jnp.where(row == 0, 0.0, pltpu.roll(a, 1, axis=0))

    for c in range(N_SLABS):
        xs = [xb_ref[0, c, pl.ds(r, n_g, stride=SUBLANES), :] for r in range(SUBLANES)]
        prev = [shift_down(xs[SUBLANES - k]) for k in range(1, CONV_WIDTH)]

        def stream(r):
            return xs[r] if r >= 0 else prev[-r - 1]

        cw = cw_ref[c]
        xr = []
        for r in range(SUBLANES):
            acc = cb_ref[c]
            for j in range(CONV_WIDTH):
                acc = acc + cw[j:j + 1, :] * stream(r - (CONV_WIDTH - 1) + j)
            xr.append(acc)
        y = _mm(jnp.concatenate(xr, axis=0), w_ref[c])
        z = -lam_ref[c]
        decay = -LRU_C * (jnp.maximum(z, 0.0) + jnp.log(1.0 + jnp.exp(-jnp.abs(z))))
        hl = None
        ac = None
        for r in range(SUBLANES):
            yr = y[r * n_g:(r + 1) * n_g]
            r_gate = _sigmoid(yr[:, :LANES] + ba_ref[c])
            i_gate = _sigmoid(yr[:, LANES:] + bx_ref[c])
            log_a = r_gate * decay
            a = jnp.exp(log_a)
            u = jnp.sqrt(-jnp.tanh(log_a) * (a * a + 1.0)) * (i_gate * xr[r])
            hl = u if r == 0 else a * hl + u
            ac = a if r == 0 else a * ac
            hl_scr[c, r] = hl
            ac_scr[c, r] = ac
        c_scr[c, 0:SUBLANES, :] = jnp.zeros((SUBLANES, LANES), jnp.float32)

    last = SUBLANES - 1

    def carry_step(g, hs):
        new = []
        for c in range(N_SLABS):
            h = hl_scr[c, last, pl.ds(g, 1), :] + ac_scr[c, last, pl.ds(g, 1), :] * hs[c]
            c_scr[c, pl.ds(g + 1, 1), :] = h
            new.append(h)
        return tuple(new)

    zero = jnp.zeros((1, LANES), jnp.float32)
    lax.fori_loop(0, n_g, carry_step, (zero,) * N_SLABS, unroll=8)

    k0 = math.sqrt(2.0 / math.pi)
    for c in range(N_SLABS):
        h_in = c_scr[c, 0:n_g, :]
        for r in range(SUBLANES):
            h = hl_scr[c, r] + ac_scr[c, r] * h_in
            gb = gb_ref[0, c, pl.ds(r, n_g, stride=SUBLANES), :]
            gelu = 0.5 * gb * (1.0 + jnp.tanh(k0 * (gb + 0.044715 * (gb * gb * gb))))
            o_ref[0, c, pl.ds(r, n_g, stride=SUBLANES), :] = h * gelu


def _rglru(proj, cw, cb, w_pair, b_a, b_x, lam, *, batch, seq):
    blk = (1, N_SLABS, seq, LANES)
    n_g = seq // SUBLANES
    return pl.pallas_call(
        functools.partial(_rglru_body, seq=seq),
        grid=(batch,),
        in_specs=[pl.BlockSpec(blk, lambda b: (b, 3, 0, 0)),
                  pl.BlockSpec(blk, lambda b: (b, 4, 0, 0)),
                  _const_spec((N_SLABS, CONV_WIDTH, LANES)), _const_spec((N_SLABS, 1, LANES)),
                  _const_spec((N_SLABS, LANES, 2 * LANES)),
                  _const_spec((N_SLABS, 1, LANES)), _const_spec((N_SLABS, 1, LANES)),
                  _const_spec((N_SLABS, 1, LANES))],
        out_specs=pl.BlockSpec(blk, lambda b: (b, 0, 0, 0)),
        out_shape=jax.ShapeDtypeStruct((batch, N_SLABS, seq, LANES), jnp.float32),
        scratch_shapes=[pltpu.VMEM((N_SLABS, SUBLANES, n_g, LANES), jnp.float32),
                        pltpu.VMEM((N_SLABS, SUBLANES, n_g, LANES), jnp.float32),
                        pltpu.VMEM((N_SLABS, n_g + SUBLANES, LANES), jnp.float32)],
        compiler_params=pltpu.CompilerParams(dimension_semantics=("arbitrary",),
                                             vmem_limit_bytes=VMEM_LIMIT),
        name="rglru",
    )(proj, proj, cw, cb, w_pair, b_a, b_x, lam)


def _rope_halves_apart(w):
    d = w.shape[0]
    w = w.reshape(d, N_HEADS // QUAD, QUAD, 2, HALF)
    return w.transpose(0, 1, 3, 2, 4).reshape(d, D_ATTN)


def _rope_tables(seq):
    pos = jnp.arange(seq, dtype=jnp.float32)
    inv = ROPE_THETA ** (-jnp.arange(0, HEAD_DIM, 2, dtype=jnp.float32) / HEAD_DIM)
    ang = pos[:, None] * inv[None, :]
    return jnp.tile(jnp.cos(ang), (1, QUAD)), jnp.tile(jnp.sin(ang), (1, QUAD))


def _pair_block_diag(w):
    z = jnp.zeros((N_SLABS, REC_BLOCK, REC_BLOCK), w.dtype)
    top = jnp.concatenate([w[0::2], z], axis=2)
    bot = jnp.concatenate([z, w[1::2]], axis=2)
    return jnp.concatenate([top, bot], axis=1)


def kernel(x, ffn1_norm, ffn1_w_in, ffn1_w_out, mix_norm, w_in, conv_w, conv_b, rg_w_a, rg_b_a, rg_w_x,
           rg_b_x, rg_lambda, attn_out_norm, rec_out_norm, w_out, ffn2_norm, ffn2_w_in, ffn2_w_out,
           final_norm):
    batch, seq, d = x.shape
    depth = ffn1_norm.shape[0]
    cos_t, sin_t = _rope_tables(seq)
    h = x.reshape(batch * seq, d)
    d_ff = ffn1_w_out.shape[1]
    ffn1_w_out_wide = ffn1_w_out.reshape(depth, d_ff * d // (2 * d_ff), 2 * d_ff)
    ffn2_w_out_wide = ffn2_w_out.reshape(depth, d_ff * d // (2 * d_ff), 2 * d_ff)
    ffn1_w = (ffn1_w_in[0].astype(MXU_DTYPE), ffn1_w_out[0].astype(MXU_DTYPE))
    w_proj = w_in[0].astype(MXU_DTYPE)
    for l in range(depth):
        w_proj = jnp.concatenate([_rope_halves_apart(w_proj[:, :D_ATTN]),
                                  _rope_halves_apart(w_proj[:, D_ATTN:2 * D_ATTN]), w_proj[:, 2 * D_ATTN:]], axis=1)
        w_pair = jnp.concatenate([_pair_block_diag(rg_w_a[l]), _pair_block_diag(rg_w_x[l])], axis=2)

        def slabs(v):
            return v.reshape(N_SLABS, 1, LANES)

        h = _ffn(h, ffn1_norm[l], *ffn1_w)
        proj = _inproj(h, mix_norm[l], w_proj, cos_t, sin_t, batch=batch, seq=seq)
        casts = [(ffn2_w_in, l), (ffn2_w_out_wide, l), (w_out, l)]
        if l + 1 < depth:
            casts += [(ffn1_w_in, l + 1), (ffn1_w_out_wide, l + 1), (w_in, l + 1)]
        y_attn, conv = _attention(proj, casts, batch=batch, seq=seq)
        y_rec = _rglru(proj, conv_w[l].reshape(CONV_WIDTH, N_SLABS, LANES).transpose(1, 0, 2),
                       slabs(conv_b[l]), w_pair.astype(MXU_DTYPE), slabs(rg_b_a[l]), slabs(rg_b_x[l]),
                       slabs(rg_lambda[l]), batch=batch, seq=seq)
        h = _ffn(h, ffn2_norm[l], conv[0], conv[1].reshape(d_ff, d),
                 mix=(y_attn, y_rec, attn_out_norm[l], rec_out_norm[l], conv[2]),
                 seq=seq, final_g=final_norm if l == depth - 1 else None)
        if l + 1 < depth:
            ffn1_w = (conv[3], conv[4].reshape(d_ff, d))
            w_proj = conv[5]
    return h.reshape(batch, seq, d)
```

```python
import functools
import math

import jax
import jax.numpy as jnp
from jax import lax
from jax.experimental import pallas as pl
from jax.experimental.pallas import tpu as pltpu

HEAD_DIM = 64
HALF = HEAD_DIM // 2
N_HEADS = 8
D_ATTN = N_HEADS * HEAD_DIM
D_REC = 512
N_REC_BLOCKS = 8
REC_BLOCK = D_REC // N_REC_BLOCKS
CONV_WIDTH = 4
LRU_C = 8.0
ROPE_THETA = 10000.0
EPS = 1e-6
STEPS = 128
DILATIONS = (1, 4, 16)

LANES = 128
SUBLANES = 8
N_SLABS = D_ATTN // LANES
MXU_TILE = 256
QUAD = 4
VMEM_LIMIT = 56 * 1024 * 1024

MXU_DTYPE = jnp.bfloat16
NEG_BIG = -1e30


def _mm(a, b):
    return jnp.dot(a.astype(MXU_DTYPE), b.astype(MXU_DTYPE), preferred_element_type=jnp.float32)


def _mm_nt(a, b):
    return lax.dot_general(a.astype(MXU_DTYPE), b.astype(MXU_DTYPE), (((1,), (1,)), ((), ())),
                           preferred_element_type=jnp.float32)


def _sigmoid(x):
    return 1.0 / (1.0 + jnp.exp(-x))


def _rms(x, g):
    var = jnp.mean(x * x, axis=-1, keepdims=True)
    return x * lax.rsqrt(var + EPS) * g


def _const_spec(shape):
    return pl.BlockSpec(shape, lambda *_: (0,) * len(shape), pipeline_mode=pl.Buffered(1))


def _ffn_chunks(d_ff, n_chunks):
    tiles = d_ff // MXU_TILE
    assert tiles * MXU_TILE == d_ff
    sizes = [(tiles // n_chunks + (1 if i < tiles % n_chunks else 0)) * MXU_TILE for i in range(n_chunks)]
    starts = [sum(sizes[:i]) for i in range(n_chunks)]
    return list(zip(starts, sizes))


def _mixer_out(ya_ref, yr_ref, ga_ref, gr_ref, w_ref):
    ya = jnp.concatenate([ya_ref[0, j] for j in range(N_SLABS)], axis=1)
    yr = jnp.concatenate([yr_ref[0, j] for j in range(N_SLABS)], axis=1)
    merged = jnp.concatenate([_rms(ya, ga_ref[...]), _rms(yr, gr_ref[...])], axis=1)
    return _mm(merged, w_ref[...])


def _ffn_body(*refs, d_ff, chunks, mix, final):
    refs = list(refs)
    x = refs.pop(0)[...]
    if mix:
        x = x + _mixer_out(*refs[:5])
        del refs[:5]
    g_ref, win_ref, wout_ref = refs[:3]
    fg_ref = refs[3] if final else None
    o_ref = refs[-1]
    xn = _rms(x, g_ref[...]).astype(MXU_DTYPE)
    acc = None
    for lo, size in chunks:
        gate = _mm(xn, win_ref[:, lo:lo + size])
        up = _mm(xn, win_ref[:, d_ff + lo:d_ff + lo + size])
        act = (gate * _sigmoid(gate) * up).astype(MXU_DTYPE)
        part = _mm(act, wout_ref[lo:lo + size, :])
        acc = part if acc is None else acc + part
    y = x + 0.5 * acc
    if final:
        y = _rms(y, fg_ref[...])
    o_ref[...] = y


def _ffn(x, g, w_in, w_out, *, mix=None, seq=None, final_g=None, n_chunks=2):
    t, d = x.shape
    d_ff = w_out.shape[0]
    final = final_g is not None
    chunks = _ffn_chunks(d_ff, n_chunks)
    tm = 512 if mix is not None else 1024
    in_specs = [pl.BlockSpec((tm, d), lambda i: (i, 0))]
    args = [x]
    if mix is not None:
        y_attn, y_rec, g_attn, g_rec, w_mix = mix
        n_seq = seq // tm
        yblk = pl.BlockSpec((1, N_SLABS, tm, LANES), lambda i: (i // n_seq, 0, i % n_seq, 0))
        in_specs += [yblk, yblk, _const_spec((1, D_ATTN)), _const_spec((1, D_REC)),
                     _const_spec((D_ATTN + D_REC, d))]
        args += [y_attn, y_rec, g_attn.reshape(1, -1), g_rec.reshape(1, -1), w_mix]
    in_specs += [_const_spec((1, d)), _const_spec((d, 2 * d_ff)), _const_spec((d_ff, d))]
    args += [g.reshape(1, d), w_in, w_out]
    if final:
        in_specs.append(_const_spec((1, d)))
        args.append(final_g.reshape(1, d))
    return pl.pallas_call(
        functools.partial(_ffn_body, d_ff=d_ff, chunks=chunks, mix=mix is not None, final=final),
        grid=(t // tm,),
        in_specs=in_specs,
        out_specs=pl.BlockSpec((tm, d), lambda i: (i, 0)),
        out_shape=jax.ShapeDtypeStruct((t, d), jnp.float32),
        compiler_params=pltpu.CompilerParams(dimension_semantics=("arbitrary",),
                                             vmem_limit_bytes=VMEM_LIMIT),
        name=("mix_ffn" if mix is not None else "ffn") + ("_final" if final else ""),
    )(*args)


N_PROJ_SLABS = 5 * N_SLABS


def _inproj_body(x_ref, g_ref, w_ref, cos_ref, sin_ref, o_ref):
    xn = _rms(x_ref[...], g_ref[...]).astype(MXU_DTYPE)
    proj = _mm(xn, w_ref[...])
    c = cos_ref[...]
    s = sin_ref[...]

    def slab(j):
        return proj[:, j * LANES:(j + 1) * LANES]

    for base, scale in ((0, math.log2(math.e) / math.sqrt(HEAD_DIM)), (N_SLABS, 1.0)):
        for g in range(N_SLABS // 2):
            t1 = slab(base + 2 * g)
            t2 = slab(base + 2 * g + 1)
            o_ref[0, base + 2 * g] = (t1 * c - t2 * s) * scale
            o_ref[0, base + 2 * g + 1] = (t2 * c + t1 * s) * scale
    for j in range(2 * N_SLABS, N_PROJ_SLABS):
        o_ref[0, j] = slab(j)


def _inproj(x, g, w, cos_t, sin_t, *, batch, seq, tm=1024):
    t, d = x.shape
    n_seq = seq // tm
    return pl.pallas_call(
        _inproj_body,
        grid=(t // tm,),
        in_specs=[pl.BlockSpec((tm, d), lambda i: (i, 0)),
                  _const_spec((1, d)), _const_spec((d, N_PROJ_SLABS * LANES)),
                  pl.BlockSpec((tm, LANES), lambda i: (i % n_seq, 0)),
                  pl.BlockSpec((tm, LANES), lambda i: (i % n_seq, 0))],
        out_specs=pl.BlockSpec((1, N_PROJ_SLABS, tm, LANES), lambda i: (i // n_seq, 0, i % n_seq, 0)),
        out_shape=jax.ShapeDtypeStruct((batch, N_PROJ_SLABS, seq, LANES), jnp.float32),
        compiler_params=pltpu.CompilerParams(dimension_semantics=("arbitrary",),
                                             vmem_limit_bytes=VMEM_LIMIT),
        name="inproj",
    )(x, g.reshape(1, d), w, cos_t, sin_t)


def _rows(start, n, stride=1):
    return pl.ds(start, n) if stride == 1 else pl.ds(start, n, stride=stride)


def _load2(ref, lead, rows):
    return jnp.concatenate([ref[(lead, 0, rows, slice(None))], ref[(lead, 1, rows, slice(None))]], axis=1)


def _store2(ref, lead, rows, val):
    ref[(lead, 0, rows, slice(None))] = val[:, :LANES]
    ref[(lead, 1, rows, slice(None))] = val[:, LANES:]


MID_DIL = DILATIONS[1]
SUB_DIL = DILATIONS[2] // MID_DIL
CLS_SLOTS = 4


def _attn_body(q_ref, k_ref, v_ref, *rest, seq, n_casts):
    cast_in, o_ref, cast_out = rest[:n_casts], rest[n_casts], rest[n_casts + 1:2 * n_casts + 1]
    bias_scr, nat_scr, cls_all, far_all = rest[2 * n_casts + 1:]
    for src, dst in zip(cast_in, cast_out):
        dst[...] = src[0].astype(dst.dtype)
    w2 = 2 * LANES
    cls_len = seq // MID_DIL
    cls_blocks = cls_len // STEPS
    assert seq == DILATIONS[2] * STEPS and DILATIONS[0] == 1
    lane = lax.broadcasted_iota(jnp.int32, (SUBLANES, w2), 1)
    q_head = (lane % LANES) // HALF
    upper_head = lax.broadcasted_iota(jnp.int32, (SUBLANES, LANES), 1) >= HEAD_DIM
    groups = STEPS // SUBLANES

    def stack_heads(per_head):
        return jnp.concatenate([per_head(h, slice(a * SUBLANES, (a + 1) * SUBLANES))
                                for a in range(groups) for h in range(QUAD)], axis=0)

    def merge_heads(full):
        wide = full.shape[1] == w2
        tiles = []
        for a in range(groups):
            halves = []
            for t in range(2):
                lanes = slice(t * LANES, (t + 1) * LANES) if wide else slice(None)
                row0 = (a * QUAD + 2 * t) * SUBLANES
                halves.append(jnp.where(upper_head, full[row0 + SUBLANES:row0 + 2 * SUBLANES, lanes],
                                        full[row0:row0 + SUBLANES, lanes]))
            tiles.append(jnp.concatenate(halves, axis=1))
        return jnp.concatenate(tiles, axis=0)

    @pl.when((pl.program_id(0) == 0) & (pl.program_id(1) == 0))
    def _():
        def table(width, fn):
            i = lax.broadcasted_iota(jnp.int32, (STEPS, width), 0)
            j = lax.broadcasted_iota(jnp.int32, (STEPS, width), 1)
            b = fn(i, j)
            return stack_heads(lambda h, rows: b[rows])

        def cls_bias(i, j):
            d = (cls_len - STEPS) + i - j
            in_mid = (d >= 0) & (d <= STEPS)
            in_far = (d >= 0) & (d % SUB_DIL == 0)
            return jnp.where(in_mid & in_far, 1.0, jnp.where(in_mid | in_far, 0.0, NEG_BIG))

        def band_bias(i, j):
            d = j - i
            return jnp.where((d >= 0) & (d <= STEPS), 0.0, NEG_BIG)

        bias_scr[:, 0:cls_len] = table(cls_len, cls_bias)
        bias_scr[:, cls_len:cls_len + 2 * STEPS] = table(2 * STEPS, band_bias)
        bias_scr[:, cls_len + 2 * STEPS:cls_len + 3 * STEPS] = table(STEPS, lambda i, j: jnp.where(j <= i, 0.0, NEG_BIG))

    def unit(qb, kw, vw, b):
        qs = stack_heads(lambda h, rows: jnp.where(q_head == h, qb[rows], 0.0))
        sc = _mm_nt(qs, kw) + b
        m = jnp.max(sc, axis=1, keepdims=True)
        p = jnp.exp2(sc - m)
        s = jnp.sum(p, axis=1, keepdims=True)
        ov = _mm(p, vw)
        return (merge_heads(ov), merge_heads(jnp.broadcast_to(m, (QUAD * STEPS, LANES))),
                merge_heads(jnp.broadcast_to(s, (QUAD * STEPS, LANES))))

    def emit(scr, rows, res):
        for kind in range(3):
            _store2(scr, kind, rows, res[kind])

    near_lo = cls_len
    emit(nat_scr, _rows(0, STEPS),
         unit(_load2(q_ref, 0, _rows(0, STEPS)), _load2(k_ref, 0, _rows(0, STEPS)),
              _load2(v_ref, 0, _rows(0, STEPS)), bias_scr[:, near_lo + 2 * STEPS:near_lo + 3 * STEPS]))

    def near_block(nb, carry):
        q0 = pl.multiple_of(nb * STEPS, STEPS)
        k_rows = _rows(pl.multiple_of(q0 - STEPS, STEPS), 2 * STEPS)
        emit(nat_scr, _rows(q0, STEPS),
             unit(_load2(q_ref, 0, _rows(q0, STEPS)), _load2(k_ref, 0, k_rows), _load2(v_ref, 0, k_rows),
                  bias_scr[:, near_lo:near_lo + 2 * STEPS]))
        return carry

    lax.fori_loop(1, seq // STEPS, near_block, 0, unroll=True)

    def klass(c, slot):
        cls_scr, far_scr = cls_all.at[slot], far_all.at[slot]
        for i, ref in enumerate((q_ref, k_ref, v_ref)):
            _store2(cls_scr, i, _rows(0, cls_len), _load2(ref, 0, _rows(c, cls_len, MID_DIL)))
        for nb in range(cls_blocks):
            k_rows = _rows(0, (nb + 1) * STEPS)
            emit(far_scr, _rows(nb * STEPS, STEPS),
                 unit(_load2(cls_scr, 0, _rows(nb * STEPS, STEPS)), _load2(cls_scr, 1, k_rows),
                      _load2(cls_scr, 2, k_rows), bias_scr[:, (cls_blocks - 1 - nb) * STEPS:cls_len]))
        for nb in range(cls_blocks):
            dense = _rows(nb * STEPS, STEPS)
            nat = _rows(c + nb * (MID_DIL * STEPS), STEPS, MID_DIL)
            for sl in range(2):
                m_near, m_far = nat_scr[1, sl, nat, :], far_scr[1, sl, dense, :]
                m_all = jnp.maximum(m_near, m_far)
                w_near, w_far = jnp.exp2(m_near - m_all), jnp.exp2(m_far - m_all)
                numer = nat_scr[0, sl, nat, :] * w_near + far_scr[0, sl, dense, :] * w_far
                denom = nat_scr[2, sl, nat, :] * w_near + far_scr[2, sl, dense, :] * w_far
                o_ref[0, sl, nat, :] = numer / denom

    def klass_group(i, carry):
        for slot in range(CLS_SLOTS):
            klass(i * CLS_SLOTS + slot, slot)
        return carry

    lax.fori_loop(0, MID_DIL // CLS_SLOTS, klass_group, 0)


def _attention(proj, casts, *, batch, seq):
    n_quads = N_HEADS // QUAD
    n_steps = batch * n_quads
    blk = (1, 2, seq, LANES)
    nat = pltpu.VMEM((3, 2, seq, LANES), jnp.float32)
    cls = pltpu.VMEM((CLS_SLOTS, 3, 2, seq // MID_DIL, LANES), jnp.float32)
    cast_in_specs, cast_out_specs, cast_out_shapes = [], [], []
    for w, layer in casts:
        _, rows, cols = w.shape
        share = next(k for k in (1, 2, 4, 8) if rows * k % n_steps == 0 and (rows * k // n_steps) % (2 * SUBLANES) == 0)
        blk_rows = rows * share // n_steps
        cast_in_specs.append(pl.BlockSpec(
            (1, blk_rows, cols), lambda b, g, layer=layer, share=share: (layer, (b * n_quads + g) // share, 0)))
        cast_out_specs.append(pl.BlockSpec(
            (blk_rows, cols), lambda b, g, share=share: ((b * n_quads + g) // share, 0)))
        cast_out_shapes.append(jax.ShapeDtypeStruct((rows, cols), MXU_DTYPE))
    outs = pl.pallas_call(
        functools.partial(_attn_body, seq=seq, n_casts=len(casts)),
        grid=(batch, n_quads),
        in_specs=[pl.BlockSpec(blk, lambda b, g: (b, g, 0, 0)),
                  pl.BlockSpec(blk, lambda b, g: (b, n_quads + g, 0, 0)),
                  pl.BlockSpec(blk, lambda b, g: (b, 2 * n_quads + g, 0, 0))] + cast_in_specs,
        out_specs=[pl.BlockSpec(blk, lambda b, g: (b, g, 0, 0))] + cast_out_specs,
        out_shape=[jax.ShapeDtypeStruct((batch, N_SLABS, seq, LANES), jnp.float32)] + cast_out_shapes,
        scratch_shapes=[pltpu.VMEM((QUAD * STEPS, seq // MID_DIL + 3 * STEPS), jnp.float32), nat, cls, cls],
        compiler_params=pltpu.CompilerParams(dimension_semantics=("arbitrary", "arbitrary"),
                                             vmem_limit_bytes=VMEM_LIMIT),
        name="dilated_attn",
    )(proj, proj, proj, *[w for w, _ in casts])
    return outs[0], outs[1:]


def _rglru_body(xb_ref, gb_ref, cw_ref, cb_ref, w_ref, ba_ref, bx_ref, lam_ref, o_ref,
                hl_scr, ac_scr, c_scr, *, seq):
    n_g = seq // SUBLANES
    row = lax.broadcasted_iota(jnp.int32, (n_g, LANES), 0)

    def shift_down(a):
        return jnp.where(row == 0, 0.0, pltpu.roll(a, 1, axis=0))

    for c in range(N_SLABS):
        xs = [xb_ref[0, c, pl.ds(r, n_g, stride=SUBLANES), :] for r in range(SUBLANES)]
        prev = [shift_down(xs[SUBLANES - k]) for k in range(1, CONV_WIDTH)]

        def stream(r):
            return xs[r] if r >= 0 else prev[-r - 1]

        cw = cw_ref[c]
        xr = []
        for r in range(SUBLANES):
            acc = cb_ref[c]
            for j in range(CONV_WIDTH):
                acc = acc + cw[j:j + 1, :] * stream(r - (CONV_WIDTH - 1) + j)
            xr.append(acc)
        y = _mm(jnp.concatenate(xr, axis=0), w_ref[c])
        z = -lam_ref[c]
        decay = -LRU_C * (jnp.maximum(z, 0.0) + jnp.log(1.0 + jnp.exp(-jnp.abs(z))))
        hl = None
        ac = None
        for r in range(SUBLANES):
            yr = y[r * n_g:(r + 1) * n_g]
            r_gate = _sigmoid(yr[:, :LANES] + ba_ref[c])
            i_gate = _sigmoid(yr[:, LANES:] + bx_ref[c])
            log_a = r_gate * decay
            a = jnp.exp(log_a)
            u = jnp.sqrt(-jnp.tanh(log_a) * (a * a + 1.0)) * (i_gate * xr[r])
            hl = u if r == 0 else a * hl + u
            ac = a if r == 0 else a * ac
            hl_scr[c, r] = hl
            ac_scr[c, r] = ac
        c_scr[c, 0:SUBLANES, :] = jnp.zeros((SUBLANES, LANES), jnp.float32)

    last = SUBLANES - 1

    def carry_step(g, hs):
        new = []
        for c in range(N_SLABS):
            h = hl_scr[c, last, pl.ds(g, 1), :] + ac_scr[c, last, pl.ds(g, 1), :] * hs[c]
            c_scr[c, pl.ds(g + 1, 1), :] = h
            new.append(h)
        return tuple(new)

    zero = jnp.zeros((1, LANES), jnp.float32)
    lax.fori_loop(0, n_g, carry_step, (zero,) * N_SLABS, unroll=8)

    k0 = math.sqrt(2.0 / math.pi)
    for c in range(N_SLABS):
        h_in = c_scr[c, 0:n_g, :]
        for r in range(SUBLANES):
            h = hl_scr[c, r] + ac_scr[c, r] * h_in
            gb = gb_ref[0, c, pl.ds(r, n_g, stride=SUBLANES), :]
            gelu = 0.5 * gb * (1.0 + jnp.tanh(k0 * (gb + 0.044715 * (gb * gb * gb))))
            o_ref[0, c, pl.ds(r, n_g, stride=SUBLANES), :] = h * gelu


def _rglru(proj, cw, cb, w_pair, b_a, b_x, lam, *, batch, seq):
    blk = (1, N_SLABS, seq, LANES)
    n_g = seq // SUBLANES
    return pl.pallas_call(
        functools.partial(_rglru_body, seq=seq),
        grid=(batch,),
        in_specs=[pl.BlockSpec(blk, lambda b: (b, 3, 0, 0)),
                  pl.BlockSpec(blk, lambda b: (b, 4, 0, 0)),
                  _const_spec((N_SLABS, CONV_WIDTH, LANES)), _const_spec((N_SLABS, 1, LANES)),
                  _const_spec((N_SLABS, LANES, 2 * LANES)),
                  _const_spec((N_SLABS, 1, LANES)), _const_spec((N_SLABS, 1, LANES)),
                  _const_spec((N_SLABS, 1, LANES))],
        out_specs=pl.BlockSpec(blk, lambda b: (b, 0, 0, 0)),
        out_shape=jax.ShapeDtypeStruct((batch, N_SLABS, seq, LANES), jnp.float32),
        scratch_shapes=[pltpu.VMEM((N_SLABS, SUBLANES, n_g, LANES), jnp.float32),
                        pltpu.VMEM((N_SLABS, SUBLANES, n_g, LANES), jnp.float32),
                        pltpu.VMEM((N_SLABS, n_g + SUBLANES, LANES), jnp.float32)],
        compiler_params=pltpu.CompilerParams(dimension_semantics=("arbitrary",),
                                             vmem_limit_bytes=VMEM_LIMIT),
        name="rglru",
    )(proj, proj, cw, cb, w_pair, b_a, b_x, lam)


def _rope_halves_apart(w):
    d = w.shape[0]
    w = w.reshape(d, N_HEADS // QUAD, QUAD, 2, HALF)
    return w.transpose(0, 1, 3, 2, 4).reshape(d, D_ATTN)


def _rope_tables(seq):
    pos = jnp.arange(seq, dtype=jnp.float32)
    inv = ROPE_THETA ** (-jnp.arange(0, HEAD_DIM, 2, dtype=jnp.float32) / HEAD_DIM)
    ang = pos[:, None] * inv[None, :]
    return jnp.tile(jnp.cos(ang), (1, QUAD)), jnp.tile(jnp.sin(ang), (1, QUAD))


def _pair_block_diag(w):
    z = jnp.zeros((N_SLABS, REC_BLOCK, REC_BLOCK), w.dtype)
    top = jnp.concatenate([w[0::2], z], axis=2)
    bot = jnp.concatenate([z, w[1::2]], axis=2)
    return jnp.concatenate([top, bot], axis=1)


def kernel(x, ffn1_norm, ffn1_w_in, ffn1_w_out, mix_norm, w_in, conv_w, conv_b, rg_w_a, rg_b_a, rg_w_x,
           rg_b_x, rg_lambda, attn_out_norm, rec_out_norm, w_out, ffn2_norm, ffn2_w_in, ffn2_w_out,
           final_norm):
    batch, seq, d = x.shape
    depth = ffn1_norm.shape[0]
    cos_t, sin_t = _rope_tables(seq)
    h = x.reshape(batch * seq, d)
    ffn1_w = (ffn1_w_in[0].astype(MXU_DTYPE), ffn1_w_out[0].astype(MXU_DTYPE))
    w_proj = w_in[0].astype(MXU_DTYPE)
    for l in range(depth):
        w_proj = jnp.concatenate([_rope_halves_apart(w_proj[:, :D_ATTN]),
                                  _rope_halves_apart(w_proj[:, D_ATTN:2 * D_ATTN]), w_proj[:, 2 * D_ATTN:]], axis=1)
        w_pair = jnp.concatenate([_pair_block_diag(rg_w_a[l]), _pair_block_diag(rg_w_x[l])], axis=2)

        def slabs(v):
            return v.reshape(N_SLABS, 1, LANES)

        h = _ffn(h, ffn1_norm[l], *ffn1_w)
        proj = _inproj(h, mix_norm[l], w_proj, cos_t, sin_t, batch=batch, seq=seq)
        casts = [(ffn2_w_in, l), (ffn2_w_out, l), (w_out, l)]
        if l + 1 < depth:
            casts += [(ffn1_w_in, l + 1), (ffn1_w_out, l + 1), (w_in, l + 1)]
        y_attn, conv = _attention(proj, casts, batch=batch, seq=seq)
        y_rec = _rglru(proj, conv_w[l].reshape(CONV_WIDTH, N_SLABS, LANES).transpose(1, 0, 2),
                       slabs(conv_b[l]), w_pair.astype(MXU_DTYPE), slabs(rg_b_a[l]), slabs(rg_b_x[l]),
                       slabs(rg_lambda[l]), batch=batch, seq=seq)
        h = _ffn(h, ffn2_norm[l], conv[0], conv[1],
                 mix=(y_attn, y_rec, attn_out_norm[l], rec_out_norm[l], conv[2]),
                 seq=seq, final_g=final_norm if l == depth - 1 else None)
        if l + 1 < depth:
            ffn1_w = (conv[3], conv[4])
            w_proj = conv[5]
    return h.reshape(batch, seq, d)
```

```python
import functools
import math

import jax
import jax.numpy as jnp
from jax import lax
from jax.experimental import pallas as pl
from jax.experimental.pallas import tpu as pltpu

HEAD_DIM = 64
HALF = HEAD_DIM // 2
N_HEADS = 8
D_ATTN = N_HEADS * HEAD_DIM
D_REC = 512
N_REC_BLOCKS = 8
REC_BLOCK = D_REC // N_REC_BLOCKS
CONV_WIDTH = 4
LRU_C = 8.0
ROPE_THETA = 10000.0
EPS = 1e-6
STEPS = 128
DILATIONS = (1, 4, 16)

LANES = 128
SUBLANES = 8
N_SLABS = D_ATTN // LANES
MXU_TILE = 256
QUAD = 4
VMEM_LIMIT = 56 * 1024 * 1024
VMEM_LIMIT_MAX = 60 * 1024 * 1024

MXU_DTYPE = jnp.bfloat16
NEG_BIG = -1e30


def _mm(a, b):
    return jnp.dot(a.astype(MXU_DTYPE), b.astype(MXU_DTYPE), preferred_element_type=jnp.float32)


def _mm_nt(a, b):
    return lax.dot_general(a.astype(MXU_DTYPE), b.astype(MXU_DTYPE), (((1,), (1,)), ((), ())),
                           preferred_element_type=jnp.float32)


LOG2_E = math.log2(math.e)


def _sigmoid(x):
    return 1.0 / (1.0 + jnp.exp2(x * -LOG2_E))


def _rms(x, g):
    var = jnp.mean(x * x, axis=-1, keepdims=True)
    return x * lax.rsqrt(var + EPS) * g


def _const_spec(shape):
    return pl.BlockSpec(shape, lambda *_: (0,) * len(shape), pipeline_mode=pl.Buffered(1))


def _ffn_chunks(d_ff, n_chunks):
    tiles = d_ff // MXU_TILE
    assert tiles * MXU_TILE == d_ff
    sizes = [(tiles // n_chunks + (1 if i < tiles % n_chunks else 0)) * MXU_TILE for i in range(n_chunks)]
    starts = [sum(sizes[:i]) for i in range(n_chunks)]
    return list(zip(starts, sizes))


def _mixer_out(ya_ref, yr_ref, ga_ref, gr_ref, w_ref):
    ya = jnp.concatenate([ya_ref[0, j] for j in range(N_SLABS)], axis=1)
    yr = jnp.concatenate([yr_ref[0, j] for j in range(N_SLABS)], axis=1)
    merged = jnp.concatenate([_rms(ya, ga_ref[...]), _rms(yr, gr_ref[...])], axis=1)
    return _mm(merged, w_ref[...])


def _ffn_body(*refs, d_ff, chunks, mix, final):
    refs = list(refs)
    x = refs.pop(0)[...]
    if mix:
        x = x + _mixer_out(*refs[:5])
        del refs[:5]
    g_ref, win_ref, wout_ref = refs[:3]
    fg_ref = refs[3] if final else None
    o_ref = refs[-1]
    xn = _rms(x, g_ref[...]).astype(MXU_DTYPE)
    acc = None
    for lo, size in chunks:
        gate = _mm(xn, win_ref[:, lo:lo + size])
        up = _mm(xn, win_ref[:, d_ff + lo:d_ff + lo + size])
        act = (gate * _sigmoid(gate) * up).astype(MXU_DTYPE)
        part = _mm(act, wout_ref[lo:lo + size, :])
        acc = part if acc is None else acc + part
    y = x + 0.5 * acc
    if final:
        y = _rms(y, fg_ref[...])
    o_ref[...] = y


def _ffn(x, g, w_in, w_out, *, mix=None, seq=None, final_g=None, tm=1024):
    t, d = x.shape
    d_ff = w_out.shape[0]
    final = final_g is not None
    chunks = _ffn_chunks(d_ff, 4 if mix is not None else 2)
    vmem_limit = VMEM_LIMIT_MAX if mix is not None else VMEM_LIMIT
    in_specs = [pl.BlockSpec((tm, d), lambda i: (i, 0))]
    args = [x]
    if mix is not None:
        y_attn, y_rec, g_attn, g_rec, w_mix = mix
        n_seq = seq // tm
        yblk = pl.BlockSpec((1, N_SLABS, tm, LANES), lambda i: (i // n_seq, 0, i % n_seq, 0))
        in_specs += [yblk, yblk, _const_spec((1, D_ATTN)), _const_spec((1, D_REC)),
                     _const_spec((D_ATTN + D_REC, d))]
        args += [y_attn, y_rec, g_attn.reshape(1, -1), g_rec.reshape(1, -1), w_mix]
    in_specs += [_const_spec((1, d)), _const_spec((d, 2 * d_ff)), _const_spec((d_ff, d))]
    args += [g.reshape(1, d), w_in, w_out]
    if final:
        in_specs.append(_const_spec((1, d)))
        args.append(final_g.reshape(1, d))
    return pl.pallas_call(
        functools.partial(_ffn_body, d_ff=d_ff, chunks=chunks, mix=mix is not None, final=final),
        grid=(t // tm,),
        in_specs=in_specs,
        out_specs=pl.BlockSpec((tm, d), lambda i: (i, 0)),
        out_shape=jax.ShapeDtypeStruct((t, d), jnp.float32),
        compiler_params=pltpu.CompilerParams(dimension_semantics=("arbitrary",),
                                             vmem_limit_bytes=vmem_limit),
        name=("mix_ffn" if mix is not None else "ffn") + ("_final" if final else ""),
    )(*args)


N_PROJ_SLABS = 5 * N_SLABS


def _inproj_body(x_ref, g_ref, w_ref, cos_ref, sin_ref, o_ref):
    xn = _rms(x_ref[...], g_ref[...]).astype(MXU_DTYPE)
    proj = _mm(xn, w_ref[...])
    c = cos_ref[...]
    s = sin_ref[...]

    def slab(j):
        return proj[:, j * LANES:(j + 1) * LANES]

    for base, scale in ((0, math.log2(math.e) / math.sqrt(HEAD_DIM)), (N_SLABS, 1.0)):
        for g in range(N_SLABS // 2):
            t1 = slab(base + 2 * g)
            t2 = slab(base + 2 * g + 1)
            o_ref[0, base + 2 * g] = (t1 * c - t2 * s) * scale
            o_ref[0, base + 2 * g + 1] = (t2 * c + t1 * s) * scale
    for j in range(2 * N_SLABS, N_PROJ_SLABS):
        o_ref[0, j] = slab(j)


def _inproj(x, g, w, cos_t, sin_t, *, batch, seq, tm=1024):
    t, d = x.shape
    n_seq = seq // tm
    return pl.pallas_call(
        _inproj_body,
        grid=(t // tm,),
        in_specs=[pl.BlockSpec((tm, d), lambda i: (i, 0)),
                  _const_spec((1, d)), _const_spec((d, N_PROJ_SLABS * LANES)),
                  pl.BlockSpec((tm, LANES), lambda i: (i % n_seq, 0)),
                  pl.BlockSpec((tm, LANES), lambda i: (i % n_seq, 0))],
        out_specs=pl.BlockSpec((1, N_PROJ_SLABS, tm, LANES), lambda i: (i // n_seq, 0, i % n_seq, 0)),
        out_shape=jax.ShapeDtypeStruct((batch, N_PROJ_SLABS, seq, LANES), jnp.float32),
        compiler_params=pltpu.CompilerParams(dimension_semantics=("arbitrary",),
                                             vmem_limit_bytes=VMEM_LIMIT),
        name="inproj",
    )(x, g.reshape(1, d), w, cos_t, sin_t)


def _rows(start, n, stride=1):
    return pl.ds(start, n) if stride == 1 else pl.ds(start, n, stride=stride)


def _load2(ref, lead, rows):
    return jnp.concatenate([ref[(lead, 0, rows, slice(None))], ref[(lead, 1, rows, slice(None))]], axis=1)


def _store2(ref, lead, rows, val):
    ref[(lead, 0, rows, slice(None))] = val[:, :LANES]
    ref[(lead, 1, rows, slice(None))] = val[:, LANES:]


MID_DIL = DILATIONS[1]
SUB_DIL = DILATIONS[2] // MID_DIL
CLS_SLOTS = 4


def _attn_body(q_ref, k_ref, v_ref, *rest, seq, n_casts):
    cast_in, o_ref, cast_out = rest[:n_casts], rest[n_casts], rest[n_casts + 1:2 * n_casts + 1]
    bias_scr, nat_scr, q_all, kv_all, far_all = rest[2 * n_casts + 1:]
    for src, dst in zip(cast_in, cast_out):
        dst[...] = src[0].astype(dst.dtype)
    w2 = 2 * LANES
    cls_len = seq // MID_DIL
    cls_blocks = cls_len // STEPS
    assert seq == DILATIONS[2] * STEPS and DILATIONS[0] == 1
    lane = lax.broadcasted_iota(jnp.int32, (SUBLANES, w2), 1)
    q_head = (lane % LANES) // HALF
    upper_head = lax.broadcasted_iota(jnp.int32, (SUBLANES, LANES), 1) >= HEAD_DIM
    groups = STEPS // SUBLANES

    def stack_heads(per_head):
        return jnp.concatenate([per_head(h, slice(a * SUBLANES, (a + 1) * SUBLANES))
                                for a in range(groups) for h in range(QUAD)], axis=0)

    def merge_heads(full):
        wide = full.shape[1] == w2
        tiles = []
        for a in range(groups):
            halves = []
            for t in range(2):
                lanes = slice(t * LANES, (t + 1) * LANES) if wide else slice(None)
                row0 = (a * QUAD + 2 * t) * SUBLANES
                halves.append(jnp.where(upper_head, full[row0 + SUBLANES:row0 + 2 * SUBLANES, lanes],
                                        full[row0:row0 + SUBLANES, lanes]))
            tiles.append(jnp.concatenate(halves, axis=1))
        return jnp.concatenate(tiles, axis=0)

    @pl.when((pl.program_id(0) == 0) & (pl.program_id(1) == 0))
    def _():
        def table(width, fn):
            i = lax.broadcasted_iota(jnp.int32, (STEPS, width), 0)
            j = lax.broadcasted_iota(jnp.int32, (STEPS, width), 1)
            b = fn(i, j)
            return stack_heads(lambda h, rows: b[rows])

        def cls_bias(i, j):
            d = (cls_len - STEPS) + i - j
            in_mid = (d >= 0) & (d <= STEPS)
            in_far = (d >= 0) & (d % SUB_DIL == 0)
            return jnp.where(in_mid & in_far, 1.0, jnp.where(in_mid | in_far, 0.0, NEG_BIG))

        def band_bias(i, j):
            d = j - i
            return jnp.where((d >= 0) & (d <= STEPS), 0.0, NEG_BIG)

        bias_scr[:, 0:cls_len] = table(cls_len, cls_bias)
        bias_scr[:, cls_len:cls_len + 2 * STEPS] = table(2 * STEPS, band_bias)
        bias_scr[:, cls_len + 2 * STEPS:cls_len + 3 * STEPS] = table(STEPS, lambda i, j: jnp.where(j <= i, 0.0, NEG_BIG))

    def unit(qb, kw, vw, b):
        qs = stack_heads(lambda h, rows: jnp.where(q_head == h, qb[rows], 0.0))
        sc = _mm_nt(qs, kw) + b
        m = jnp.max(sc, axis=1, keepdims=True)
        p = jnp.exp2(sc - m)
        s = jnp.sum(p, axis=1, keepdims=True)
        ov = _mm(p, vw)
        return (merge_heads(ov), merge_heads(jnp.broadcast_to(m, (QUAD * STEPS, LANES))),
                merge_heads(jnp.broadcast_to(s, (QUAD * STEPS, LANES))))

    def emit(scr, rows, res):
        for kind in range(3):
            _store2(scr, kind, rows, res[kind])

    near_lo = cls_len
    emit(nat_scr, _rows(0, STEPS),
         unit(_load2(q_ref, 0, _rows(0, STEPS)), _load2(k_ref, 0, _rows(0, STEPS)),
              _load2(v_ref, 0, _rows(0, STEPS)), bias_scr[:, near_lo + 2 * STEPS:near_lo + 3 * STEPS]))

    def near_block(nb, carry):
        q0 = pl.multiple_of(nb * STEPS, STEPS)
        k_rows = _rows(pl.multiple_of(q0 - STEPS, STEPS), 2 * STEPS)
        emit(nat_scr, _rows(q0, STEPS),
             unit(_load2(q_ref, 0, _rows(q0, STEPS)), _load2(k_ref, 0, k_rows), _load2(v_ref, 0, k_rows),
                  bias_scr[:, near_lo:near_lo + 2 * STEPS]))
        return carry

    lax.fori_loop(1, seq // STEPS, near_block, 0, unroll=True)

    def klass(c, slot):
        q_scr, kv_scr, far_scr = q_all.at[slot], kv_all.at[slot], far_all.at[slot]
        cls_rows = _rows(c, cls_len, MID_DIL)
        _store2(q_scr, 0, _rows(0, cls_len), _load2(q_ref, 0, cls_rows))
        for i, ref in enumerate((k_ref, v_ref)):
            _store2(kv_scr, i, _rows(0, cls_len), _load2(ref, 0, cls_rows).astype(MXU_DTYPE))
        for nb in range(cls_blocks):
            k_rows = _rows(0, (nb + 1) * STEPS)
            emit(far_scr, _rows(nb * STEPS, STEPS),
                 unit(_load2(q_scr, 0, _rows(nb * STEPS, STEPS)), _load2(kv_scr, 0, k_rows),
                      _load2(kv_scr, 1, k_rows), bias_scr[:, (cls_blocks - 1 - nb) * STEPS:cls_len]))
        for nb in range(cls_blocks):
            dense = _rows(nb * STEPS, STEPS)
            nat = _rows(c + nb * (MID_DIL * STEPS), STEPS, MID_DIL)
            for sl in range(2):
                m_near, m_far = nat_scr[1, sl, nat, :], far_scr[1, sl, dense, :]
                m_all = jnp.maximum(m_near, m_far)
                w_near, w_far = jnp.exp2(m_near - m_all), jnp.exp2(m_far - m_all)
                numer = nat_scr[0, sl, nat, :] * w_near + far_scr[0, sl, dense, :] * w_far
                denom = nat_scr[2, sl, nat, :] * w_near + far_scr[2, sl, dense, :] * w_far
                o_ref[0, sl, nat, :] = numer / denom

    def klass_group(i, carry):
        for slot in range(CLS_SLOTS):
            klass(i * CLS_SLOTS + slot, slot)
        return carry

    lax.fori_loop(0, MID_DIL // CLS_SLOTS, klass_group, 0)


def _attention(proj, casts, *, batch, seq):
    n_quads = N_HEADS // QUAD
    n_steps = batch * n_quads
    blk = (1, 2, seq, LANES)
    nat = pltpu.VMEM((3, 2, seq, LANES), jnp.float32)
    cls_q = pltpu.VMEM((CLS_SLOTS, 1, 2, seq // MID_DIL, LANES), jnp.float32)
    cls_kv = pltpu.VMEM((CLS_SLOTS, 2, 2, seq // MID_DIL, LANES), MXU_DTYPE)
    cls_out = pltpu.VMEM((CLS_SLOTS, 3, 2, seq // MID_DIL, LANES), jnp.float32)
    cast_in_specs, cast_out_specs, cast_out_shapes = [], [], []
    for w, layer in casts:
        _, rows, cols = w.shape
        share = next(k for k in (1, 2, 4, 8) if rows * k % n_steps == 0 and (rows * k // n_steps) % (2 * SUBLANES) == 0)
        blk_rows = rows * share // n_steps
        cast_in_specs.append(pl.BlockSpec(
            (1, blk_rows, cols), lambda b, g, layer=layer, share=share: (layer, (b * n_quads + g) // share, 0)))
        cast_out_specs.append(pl.BlockSpec(
            (blk_rows, cols), lambda b, g, share=share: ((b * n_quads + g) // share, 0)))
        cast_out_shapes.append(jax.ShapeDtypeStruct((rows, cols), MXU_DTYPE))
    outs = pl.pallas_call(
        functools.partial(_attn_body, seq=seq, n_casts=len(casts)),
        grid=(batch, n_quads),
        in_specs=[pl.BlockSpec(blk, lambda b, g: (b, g, 0, 0)),
                  pl.BlockSpec(blk, lambda b, g: (b, n_quads + g, 0, 0)),
                  pl.BlockSpec(blk, lambda b, g: (b, 2 * n_quads + g, 0, 0))] + cast_in_specs,
        out_specs=[pl.BlockSpec(blk, lambda b, g: (b, g, 0, 0))] + cast_out_specs,
        out_shape=[jax.ShapeDtypeStruct((batch, N_SLABS, seq, LANES), jnp.float32)] + cast_out_shapes,
        scratch_shapes=[pltpu.VMEM((QUAD * STEPS, seq // MID_DIL + 3 * STEPS), jnp.float32), nat,
                        cls_q, cls_kv, cls_out],
        compiler_params=pltpu.CompilerParams(dimension_semantics=("arbitrary", "arbitrary"),
                                             vmem_limit_bytes=VMEM_LIMIT),
        name="dilated_attn",
    )(proj, proj, proj, *[w for w, _ in casts])
    return outs[0], outs[1:]


def _rglru_body(xb_ref, gb_ref, cw_ref, cb_ref, w_ref, ba_ref, bx_ref, lam_ref, o_ref,
                hl_scr, ac_scr, c_scr, *, seq):
    n_g = seq // SUBLANES
    row = lax.broadcasted_iota(jnp.int32, (n_g, LANES), 0)

    def shift_down(a):
        return jnp.where(row == 0, 0.0, pltpu.roll(a, 1, axis=0))

    for c in range(N_SLABS):
        xs = [xb_ref[0, c, pl.ds(r, n_g, stride=SUBLANES), :] for r in range(SUBLANES)]
        prev = [shift_down(xs[SUBLANES - k]) for k in range(1, CONV_WIDTH)]

        def stream(r):
            return xs[r] if r >= 0 else prev[-r - 1]

        cw = cw_ref[c]
        xr = []
        for r in range(SUBLANES):
            acc = cb_ref[c]
            for j in range(CONV_WIDTH):
                acc = acc + cw[j:j + 1, :] * stream(r - (CONV_WIDTH - 1) + j)
            xr.append(acc)
        y = _mm(jnp.concatenate(xr, axis=0), w_ref[c])
        z = -lam_ref[c]
        decay = LRU_C * (jnp.maximum(z, 0.0) + jnp.log(1.0 + jnp.exp(-jnp.abs(z))))
        hl = None
        ac = None
        for r in range(SUBLANES):
            yr = y[r * n_g:(r + 1) * n_g]
            r_gate = _sigmoid(yr[:, :LANES] + ba_ref[c])
            i_gate = _sigmoid(yr[:, LANES:] + bx_ref[c])
            neg_log_a = r_gate * decay
            a = jnp.exp2(neg_log_a * -LOG2_E)
            u = jnp.sqrt(jnp.tanh(neg_log_a) * (a * a + 1.0)) * (i_gate * xr[r])
            hl = u if r == 0 else a * hl + u
            ac = a if r == 0 else a * ac
            hl_scr[c, r] = hl
            ac_scr[c, r] = ac
        c_scr[c, 0:SUBLANES, :] = jnp.zeros((SUBLANES, LANES), jnp.float32)

    last = SUBLANES - 1

    def carry_step(g, hs):
        new = []
        for c in range(N_SLABS):
            h = hl_scr[c, last, pl.ds(g, 1), :] + ac_scr[c, last, pl.ds(g, 1), :] * hs[c]
            c_scr[c, pl.ds(g + 1, 1), :] = h
            new.append(h)
        return tuple(new)

    zero = jnp.zeros((1, LANES), jnp.float32)
    lax.fori_loop(0, n_g, carry_step, (zero,) * N_SLABS, unroll=8)

    k0 = math.sqrt(2.0 / math.pi)
    for c in range(N_SLABS):
        h_in = c_scr[c, 0:n_g, :]
        for r in range(SUBLANES):
            h = hl_scr[c, r] + ac_scr[c, r] * h_in
            gb = gb_ref[0, c, pl.ds(r, n_g, stride=SUBLANES), :]
            gate = gb * (0.5 + 0.5 * jnp.tanh(gb * (k0 + (k0 * 0.044715) * (gb * gb))))
            o_ref[0, c, pl.ds(r, n_g, stride=SUBLANES), :] = h * gate


def _rglru(proj, cw, cb, w_pair, b_a, b_x, lam, *, batch, seq):
    blk = (1, N_SLABS, seq, LANES)
    n_g = seq // SUBLANES
    return pl.pallas_call(
        functools.partial(_rglru_body, seq=seq),
        grid=(batch,),
        in_specs=[pl.BlockSpec(blk, lambda b: (b, 3, 0, 0)),
                  pl.BlockSpec(blk, lambda b: (b, 4, 0, 0)),
                  _const_spec((N_SLABS, CONV_WIDTH, LANES)), _const_spec((N_SLABS, 1, LANES)),
                  _const_spec((N_SLABS, LANES, 2 * LANES)),
                  _const_spec((N_SLABS, 1, LANES)), _const_spec((N_SLABS, 1, LANES)),
                  _const_spec((N_SLABS, 1, LANES))],
        out_specs=pl.BlockSpec(blk, lambda b: (b, 0, 0, 0)),
        out_shape=jax.ShapeDtypeStruct((batch, N_SLABS, seq, LANES), jnp.float32),
        scratch_shapes=[pltpu.VMEM((N_SLABS, SUBLANES, n_g, LANES), jnp.float32),
                        pltpu.VMEM((N_SLABS, SUBLANES, n_g, LANES), jnp.float32),
                        pltpu.VMEM((N_SLABS, n_g + SUBLANES, LANES), jnp.float32)],
        compiler_params=pltpu.CompilerParams(dimension_semantics=("arbitrary",),
                                             vmem_limit_bytes=VMEM_LIMIT),
        name="rglru",
    )(proj, proj, cw, cb, w_pair, b_a, b_x, lam)


def _rope_halves_apart(w):
    d = w.shape[0]
    w = w.reshape(d, N_HEADS // QUAD, QUAD, 2, HALF)
    return w.transpose(0, 1, 3, 2, 4).reshape(d, D_ATTN)


def _rope_tables(seq):
    pos = jnp.arange(seq, dtype=jnp.float32)
    inv = ROPE_THETA ** (-jnp.arange(0, HEAD_DIM, 2, dtype=jnp.float32) / HEAD_DIM)
    ang = pos[:, None] * inv[None, :]
    return jnp.tile(jnp.cos(ang), (1, QUAD)), jnp.tile(jnp.sin(ang), (1, QUAD))


def _pair_block_diag(w):
    z = jnp.zeros((N_SLABS, REC_BLOCK, REC_BLOCK), w.dtype)
    top = jnp.concatenate([w[0::2], z], axis=2)
    bot = jnp.concatenate([z, w[1::2]], axis=2)
    return jnp.concatenate([top, bot], axis=1)


def kernel(x, ffn1_norm, ffn1_w_in, ffn1_w_out, mix_norm, w_in, conv_w, conv_b, rg_w_a, rg_b_a, rg_w_x,
           rg_b_x, rg_lambda, attn_out_norm, rec_out_norm, w_out, ffn2_norm, ffn2_w_in, ffn2_w_out,
           final_norm):
    batch, seq, d = x.shape
    depth = ffn1_norm.shape[0]
    cos_t, sin_t = _rope_tables(seq)
    h = x.reshape(batch * seq, d)
    ffn1_w = (ffn1_w_in[0].astype(MXU_DTYPE), ffn1_w_out[0].astype(MXU_DTYPE))
    w_proj = w_in[0].astype(MXU_DTYPE)
    for l in range(depth):
        w_proj = jnp.concatenate([_rope_halves_apart(w_proj[:, :D_ATTN]),
                                  _rope_halves_apart(w_proj[:, D_ATTN:2 * D_ATTN]), w_proj[:, 2 * D_ATTN:]], axis=1)
        w_pair = jnp.concatenate([_pair_block_diag(rg_w_a[l]), _pair_block_diag(rg_w_x[l])], axis=2)

        def slabs(v):
            return v.reshape(N_SLABS, 1, LANES)

        h = _ffn(h, ffn1_norm[l], *ffn1_w)
        proj = _inproj(h, mix_norm[l], w_proj, cos_t, sin_t, batch=batch, seq=seq)
        casts = [(ffn2_w_in, l), (ffn2_w_out, l), (w_out, l)]
        if l + 1 < depth:
            casts += [(ffn1_w_in, l + 1), (ffn1_w_out, l + 1), (w_in, l + 1)]
        y_attn, conv = _attention(proj, casts, batch=batch, seq=seq)
        y_rec = _rglru(proj, conv_w[l].reshape(CONV_WIDTH, N_SLABS, LANES).transpose(1, 0, 2),
                       slabs(conv_b[l]), w_pair.astype(MXU_DTYPE), slabs(rg_b_a[l]), slabs(rg_b_x[l]),
                       slabs(rg_lambda[l]), batch=batch, seq=seq)
        h = _ffn(h, ffn2_norm[l], conv[0], conv[1],
                 mix=(y_attn, y_rec, attn_out_norm[l], rec_out_norm[l], conv[2]),
                 seq=seq, final_g=final_norm if l == depth - 1 else None)
        if l + 1 < depth:
            ffn1_w = (conv[3], conv[4])
            w_proj = conv[5]
    return h.reshape(batch, seq, d)
```

```python
import functools
import math

import jax
import jax.numpy as jnp
from jax import lax
from jax.experimental import pallas as pl
from jax.experimental.pallas import tpu as pltpu

HEAD_DIM = 64
HALF = HEAD_DIM // 2
N_HEADS = 8
D_ATTN = N_HEADS * HEAD_DIM
D_REC = 512
N_REC_BLOCKS = 8
REC_BLOCK = D_REC // N_REC_BLOCKS
CONV_WIDTH = 4
LRU_C = 8.0
ROPE_THETA = 10000.0
EPS = 1e-6
STEPS = 128
DILATIONS = (1, 4, 16)

LANES = 128
SUBLANES = 8
N_SLABS = D_ATTN // LANES
MXU_TILE = 256
QUAD = 4
VMEM_LIMIT = 56 * 1024 * 1024
VMEM_LIMIT_MAX = 60 * 1024 * 1024

MXU_DTYPE = jnp.bfloat16
NEG_BIG = -1e30


def _mm(a, b):
    return jnp.dot(a.astype(MXU_DTYPE), b.astype(MXU_DTYPE), preferred_element_type=jnp.float32)


def _mm_nt(a, b):
    return lax.dot_general(a.astype(MXU_DTYPE), b.astype(MXU_DTYPE), (((1,), (1,)), ((), ())),
                           preferred_element_type=jnp.float32)


LOG2_E = math.log2(math.e)


def _sigmoid(x):
    return 1.0 / (1.0 + jnp.exp2(x * -LOG2_E))


def _rms(x, g):
    var = jnp.mean(x * x, axis=-1, keepdims=True)
    return x * lax.rsqrt(var + EPS) * g


def _const_spec(shape):
    return pl.BlockSpec(shape, lambda *_: (0,) * len(shape), pipeline_mode=pl.Buffered(1))


def _cast_specs(casts, n_steps):
    in_specs, out_specs, out_shapes = [], [], []
    for w, layer in casts:
        _, rows, cols = w.shape
        share = next(k for k in (1, 2, 4, 8)
                     if rows * k % n_steps == 0 and (rows * k // n_steps) % (2 * SUBLANES) == 0)
        blk_rows = rows * share // n_steps
        in_specs.append(pl.BlockSpec((1, blk_rows, cols), lambda i, layer=layer, share=share: (layer, i // share, 0)))
        out_specs.append(pl.BlockSpec((blk_rows, cols), lambda i, share=share: (i // share, 0)))
        out_shapes.append(jax.ShapeDtypeStruct((rows, cols), MXU_DTYPE))
    return in_specs, out_specs, out_shapes


def _convert_blocks(srcs, dsts):
    for src, dst in zip(srcs, dsts):
        dst[...] = src[0].astype(dst.dtype)


def _ffn_chunks(d_ff, n_chunks):
    tiles = d_ff // MXU_TILE
    assert tiles * MXU_TILE == d_ff
    sizes = [(tiles // n_chunks + (1 if i < tiles % n_chunks else 0)) * MXU_TILE for i in range(n_chunks)]
    starts = [sum(sizes[:i]) for i in range(n_chunks)]
    return list(zip(starts, sizes))


def _mixer_out(ya_ref, yr_ref, ga_ref, gr_ref, w_ref):
    ya = jnp.concatenate([ya_ref[0, j] for j in range(N_SLABS)], axis=1)
    yr = jnp.concatenate([yr_ref[0, j] for j in range(N_SLABS)], axis=1)
    merged = jnp.concatenate([_rms(ya, ga_ref[...]), _rms(yr, gr_ref[...])], axis=1)
    return _mm(merged, w_ref[...])


def _ffn_body(*refs, d_ff, chunks, mix, final, n_casts):
    refs = list(refs)
    x = refs.pop(0)[...]
    if mix:
        x = x + _mixer_out(*refs[:5])
        del refs[:5]
    g_ref, win_ref, wout_ref = refs[:3]
    del refs[:3]
    fg_ref = refs.pop(0) if final else None
    o_ref = refs[n_casts]
    _convert_blocks(refs[:n_casts], refs[n_casts + 1:])
    xn = _rms(x, g_ref[...]).astype(MXU_DTYPE)
    acc = None
    for lo, size in chunks:
        gate = _mm(xn, win_ref[:, lo:lo + size])
        up = _mm(xn, win_ref[:, d_ff + lo:d_ff + lo + size])
        act = (gate * _sigmoid(gate) * up).astype(MXU_DTYPE)
        part = _mm(act, wout_ref[lo:lo + size, :])
        acc = part if acc is None else acc + part
    y = x + 0.5 * acc
    if final:
        y = _rms(y, fg_ref[...])
    o_ref[...] = y


def _ffn(x, g, w_in, w_out, *, mix=None, seq=None, final_g=None, casts=(), tm=1024):
    t, d = x.shape
    d_ff = w_out.shape[0]
    final = final_g is not None
    chunks = _ffn_chunks(d_ff, 4 if mix is not None else 2)
    vmem_limit = VMEM_LIMIT_MAX if mix is not None else VMEM_LIMIT
    in_specs = [pl.BlockSpec((tm, d), lambda i: (i, 0))]
    args = [x]
    if mix is not None:
        y_attn, y_rec, g_attn, g_rec, w_mix = mix
        n_seq = seq // tm
        yblk = pl.BlockSpec((1, N_SLABS, tm, LANES), lambda i: (i // n_seq, 0, i % n_seq, 0))
        in_specs += [yblk, yblk, _const_spec((1, D_ATTN)), _const_spec((1, D_REC)),
                     _const_spec((D_ATTN + D_REC, d))]
        args += [y_attn, y_rec, g_attn.reshape(1, -1), g_rec.reshape(1, -1), w_mix]
    in_specs += [_const_spec((1, d)), _const_spec((d, 2 * d_ff)), _const_spec((d_ff, d))]
    args += [g.reshape(1, d), w_in, w_out]
    if final:
        in_specs.append(_const_spec((1, d)))
        args.append(final_g.reshape(1, d))
    cast_in, cast_out, cast_shapes = _cast_specs(casts, t // tm)
    outs = pl.pallas_call(
        functools.partial(_ffn_body, d_ff=d_ff, chunks=chunks, mix=mix is not None, final=final,
                          n_casts=len(casts)),
        grid=(t // tm,),
        in_specs=in_specs + cast_in,
        out_specs=[pl.BlockSpec((tm, d), lambda i: (i, 0))] + cast_out,
        out_shape=[jax.ShapeDtypeStruct((t, d), jnp.float32)] + cast_shapes,
        compiler_params=pltpu.CompilerParams(dimension_semantics=("arbitrary",),
                                             vmem_limit_bytes=vmem_limit),
        name=("mix_ffn" if mix is not None else "ffn") + ("_final" if final else ""),
    )(*args, *[w for w, _ in casts])
    return outs[0], outs[1:]


N_PROJ_SLABS = 5 * N_SLABS


def _inproj_body(x_ref, g_ref, w_ref, cos_ref, sin_ref, *rest, n_casts):
    o_ref = rest[n_casts]
    _convert_blocks(rest[:n_casts], rest[n_casts + 1:])
    xn = _rms(x_ref[...], g_ref[...]).astype(MXU_DTYPE)
    proj = _mm(xn, w_ref[...])
    c = cos_ref[...]
    s = sin_ref[...]

    def slab(j):
        return proj[:, j * LANES:(j + 1) * LANES]

    for base, scale in ((0, math.log2(math.e) / math.sqrt(HEAD_DIM)), (N_SLABS, 1.0)):
        for g in range(N_SLABS // 2):
            t1 = slab(base + 2 * g)
            t2 = slab(base + 2 * g + 1)
            o_ref[0, base + 2 * g] = (t1 * c - t2 * s) * scale
            o_ref[0, base + 2 * g + 1] = (t2 * c + t1 * s) * scale
    for j in range(2 * N_SLABS, N_PROJ_SLABS):
        o_ref[0, j] = slab(j)


def _inproj(x, g, w, cos_t, sin_t, *, batch, seq, casts=(), tm=1024):
    t, d = x.shape
    n_seq = seq // tm
    cast_in, cast_out, cast_shapes = _cast_specs(casts, t // tm)
    outs = pl.pallas_call(
        functools.partial(_inproj_body, n_casts=len(casts)),
        grid=(t // tm,),
        in_specs=[pl.BlockSpec((tm, d), lambda i: (i, 0)),
                  _const_spec((1, d)), _const_spec((d, N_PROJ_SLABS * LANES)),
                  pl.BlockSpec((tm, LANES), lambda i: (i % n_seq, 0)),
                  pl.BlockSpec((tm, LANES), lambda i: (i % n_seq, 0))] + cast_in,
        out_specs=[pl.BlockSpec((1, N_PROJ_SLABS, tm, LANES), lambda i: (i // n_seq, 0, i % n_seq, 0))] + cast_out,
        out_shape=[jax.ShapeDtypeStruct((batch, N_PROJ_SLABS, seq, LANES), jnp.float32)] + cast_shapes,
        compiler_params=pltpu.CompilerParams(dimension_semantics=("arbitrary",),
                                             vmem_limit_bytes=VMEM_LIMIT),
        name="inproj",
    )(x, g.reshape(1, d), w, cos_t, sin_t, *[w_ for w_, _ in casts])
    return outs[0], outs[1:]


def _rows(start, n, stride=1):
    return pl.ds(start, n) if stride == 1 else pl.ds(start, n, stride=stride)


def _load2(ref, lead, rows):
    return jnp.concatenate([ref[(lead, 0, rows, slice(None))], ref[(lead, 1, rows, slice(None))]], axis=1)


def _store2(ref, lead, rows, val):
    ref[(lead, 0, rows, slice(None))] = val[:, :LANES]
    ref[(lead, 1, rows, slice(None))] = val[:, LANES:]


MID_DIL = DILATIONS[1]
SUB_DIL = DILATIONS[2] // MID_DIL
CLS_SLOTS = 4


def _attn_body(q_ref, k_ref, v_ref, o_ref, bias_scr, nat_scr, q_all, kv_all, far_all, *, seq):
    w2 = 2 * LANES
    cls_len = seq // MID_DIL
    cls_blocks = cls_len // STEPS
    assert seq == DILATIONS[2] * STEPS and DILATIONS[0] == 1
    lane = lax.broadcasted_iota(jnp.int32, (SUBLANES, w2), 1)
    q_head = (lane % LANES) // HALF
    upper_head = lax.broadcasted_iota(jnp.int32, (SUBLANES, LANES), 1) >= HEAD_DIM
    groups = STEPS // SUBLANES

    def stack_heads(per_head):
        return jnp.concatenate([per_head(h, slice(a * SUBLANES, (a + 1) * SUBLANES))
                                for a in range(groups) for h in range(QUAD)], axis=0)

    def merge_heads(full):
        wide = full.shape[1] == w2
        tiles = []
        for a in range(groups):
            halves = []
            for t in range(2):
                lanes = slice(t * LANES, (t + 1) * LANES) if wide else slice(None)
                row0 = (a * QUAD + 2 * t) * SUBLANES
                halves.append(jnp.where(upper_head, full[row0 + SUBLANES:row0 + 2 * SUBLANES, lanes],
                                        full[row0:row0 + SUBLANES, lanes]))
            tiles.append(jnp.concatenate(halves, axis=1))
        return jnp.concatenate(tiles, axis=0)

    @pl.when((pl.program_id(0) == 0) & (pl.program_id(1) == 0))
    def _():
        def table(width, fn):
            i = lax.broadcasted_iota(jnp.int32, (STEPS, width), 0)
            j = lax.broadcasted_iota(jnp.int32, (STEPS, width), 1)
            b = fn(i, j)
            return stack_heads(lambda h, rows: b[rows])

        def cls_bias(i, j):
            d = (cls_len - STEPS) + i - j
            in_mid = (d >= 0) & (d <= STEPS)
            in_far = (d >= 0) & (d % SUB_DIL == 0)
            return jnp.where(in_mid & in_far, 1.0, jnp.where(in_mid | in_far, 0.0, NEG_BIG))

        def band_bias(i, j):
            d = j - i
            return jnp.where((d >= 0) & (d <= STEPS), 0.0, NEG_BIG)

        bias_scr[:, 0:cls_len] = table(cls_len, cls_bias)
        bias_scr[:, cls_len:cls_len + 2 * STEPS] = table(2 * STEPS, band_bias)
        bias_scr[:, cls_len + 2 * STEPS:cls_len + 3 * STEPS] = table(STEPS, lambda i, j: jnp.where(j <= i, 0.0, NEG_BIG))

    def unit(qb, kw, vw, b):
        qs = stack_heads(lambda h, rows: jnp.where(q_head == h, qb[rows], 0.0))
        sc = _mm_nt(qs, kw) + b
        m = jnp.max(sc, axis=1, keepdims=True)
        p = jnp.exp2(sc - m)
        s = jnp.sum(p, axis=1, keepdims=True)
        ov = _mm(p, vw)
        return (merge_heads(ov), merge_heads(jnp.broadcast_to(m, (QUAD * STEPS, LANES))),
                merge_heads(jnp.broadcast_to(s, (QUAD * STEPS, LANES))))

    def emit(scr, rows, res):
        for kind in range(3):
            _store2(scr, kind, rows, res[kind])

    near_lo = cls_len
    emit(nat_scr, _rows(0, STEPS),
         unit(_load2(q_ref, 0, _rows(0, STEPS)), _load2(k_ref, 0, _rows(0, STEPS)),
              _load2(v_ref, 0, _rows(0, STEPS)), bias_scr[:, near_lo + 2 * STEPS:near_lo + 3 * STEPS]))

    def near_block(nb, carry):
        q0 = pl.multiple_of(nb * STEPS, STEPS)
        k_rows = _rows(pl.multiple_of(q0 - STEPS, STEPS), 2 * STEPS)
        emit(nat_scr, _rows(q0, STEPS),
             unit(_load2(q_ref, 0, _rows(q0, STEPS)), _load2(k_ref, 0, k_rows), _load2(v_ref, 0, k_rows),
                  bias_scr[:, near_lo:near_lo + 2 * STEPS]))
        return carry

    lax.fori_loop(1, seq // STEPS, near_block, 0, unroll=True)

    def klass(c, slot):
        q_scr, kv_scr, far_scr = q_all.at[slot], kv_all.at[slot], far_all.at[slot]
        cls_rows = _rows(c, cls_len, MID_DIL)
        _store2(q_scr, 0, _rows(0, cls_len), _load2(q_ref, 0, cls_rows))
        for i, ref in enumerate((k_ref, v_ref)):
            _store2(kv_scr, i, _rows(0, cls_len), _load2(ref, 0, cls_rows).astype(MXU_DTYPE))
        for nb in range(cls_blocks):
            k_rows = _rows(0, (nb + 1) * STEPS)
            emit(far_scr, _rows(nb * STEPS, STEPS),
                 unit(_load2(q_scr, 0, _rows(nb * STEPS, STEPS)), _load2(kv_scr, 0, k_rows),
                      _load2(kv_scr, 1, k_rows), bias_scr[:, (cls_blocks - 1 - nb) * STEPS:cls_len]))
        for nb in range(cls_blocks):
            dense = _rows(nb * STEPS, STEPS)
            nat = _rows(c + nb * (MID_DIL * STEPS), STEPS, MID_DIL)
            for sl in range(2):
                m_near, m_far = nat_scr[1, sl, nat, :], far_scr[1, sl, dense, :]
                m_all = jnp.maximum(m_near, m_far)
                w_near, w_far = jnp.exp2(m_near - m_all), jnp.exp2(m_far - m_all)
                numer = nat_scr[0, sl, nat, :] * w_near + far_scr[0, sl, dense, :] * w_far
                denom = nat_scr[2, sl, nat, :] * w_near + far_scr[2, sl, dense, :] * w_far
                o_ref[0, sl, nat, :] = numer / denom

    def klass_group(i, carry):
        for slot in range(CLS_SLOTS):
            klass(i * CLS_SLOTS + slot, slot)
        return carry

    lax.fori_loop(0, MID_DIL // CLS_SLOTS, klass_group, 0)


def _attention(proj, *, batch, seq):
    n_quads = N_HEADS // QUAD
    blk = (1, 2, seq, LANES)
    nat = pltpu.VMEM((3, 2, seq, LANES), jnp.float32)
    cls_q = pltpu.VMEM((CLS_SLOTS, 1, 2, seq // MID_DIL, LANES), jnp.float32)
    cls_kv = pltpu.VMEM((CLS_SLOTS, 2, 2, seq // MID_DIL, LANES), MXU_DTYPE)
    cls_out = pltpu.VMEM((CLS_SLOTS, 3, 2, seq // MID_DIL, LANES), jnp.float32)
    return pl.pallas_call(
        functools.partial(_attn_body, seq=seq),
        grid=(batch, n_quads),
        in_specs=[pl.BlockSpec(blk, lambda b, g: (b, g, 0, 0)),
                  pl.BlockSpec(blk, lambda b, g: (b, n_quads + g, 0, 0)),
                  pl.BlockSpec(blk, lambda b, g: (b, 2 * n_quads + g, 0, 0))],
        out_specs=pl.BlockSpec(blk, lambda b, g: (b, g, 0, 0)),
        out_shape=jax.ShapeDtypeStruct((batch, N_SLABS, seq, LANES), jnp.float32),
        scratch_shapes=[pltpu.VMEM((QUAD * STEPS, seq // MID_DIL + 3 * STEPS), jnp.float32), nat,
                        cls_q, cls_kv, cls_out],
        compiler_params=pltpu.CompilerParams(dimension_semantics=("arbitrary", "arbitrary"),
                                             vmem_limit_bytes=VMEM_LIMIT),
        name="dilated_attn",
    )(proj, proj, proj)


def _rglru_body(xb_ref, gb_ref, cw_ref, cb_ref, w_ref, ba_ref, bx_ref, lam_ref, o_ref,
                hl_scr, ac_scr, c_scr, *, seq):
    n_g = seq // SUBLANES
    row = lax.broadcasted_iota(jnp.int32, (n_g, LANES), 0)

    def shift_down(a):
        return jnp.where(row == 0, 0.0, pltpu.roll(a, 1, axis=0))

    for c in range(N_SLABS):
        xs = [xb_ref[0, c, pl.ds(r, n_g, stride=SUBLANES), :] for r in range(SUBLANES)]
        prev = [shift_down(xs[SUBLANES - k]) for k in range(1, CONV_WIDTH)]

        def stream(r):
            return xs[r] if r >= 0 else prev[-r - 1]

        cw = cw_ref[c]
        xr = []
        for r in range(SUBLANES):
            acc = cb_ref[c]
            for j in range(CONV_WIDTH):
                acc = acc + cw[j:j + 1, :] * stream(r - (CONV_WIDTH - 1) + j)
            xr.append(acc)
        y = _mm(jnp.concatenate(xr, axis=0), w_ref[c])
        z = -lam_ref[c]
        decay = LRU_C * (jnp.maximum(z, 0.0) + jnp.log(1.0 + jnp.exp(-jnp.abs(z))))
        hl = None
        ac = None
        for r in range(SUBLANES):
            yr = y[r * n_g:(r + 1) * n_g]
            r_gate = _sigmoid(yr[:, :LANES] + ba_ref[c])
            i_gate = _sigmoid(yr[:, LANES:] + bx_ref[c])
            neg_log_a = r_gate * decay
            a = jnp.exp2(neg_log_a * -LOG2_E)
            u = jnp.sqrt(jnp.tanh(neg_log_a) * (a * a + 1.0)) * (i_gate * xr[r])
            hl = u if r == 0 else a * hl + u
            ac = a if r == 0 else a * ac
            hl_scr[c, r] = hl
            ac_scr[c, r] = ac
        c_scr[c, 0:SUBLANES, :] = jnp.zeros((SUBLANES, LANES), jnp.float32)

    last = SUBLANES - 1

    def carry_step(g, hs):
        new = []
        for c in range(N_SLABS):
            h = hl_scr[c, last, pl.ds(g, 1), :] + ac_scr[c, last, pl.ds(g, 1), :] * hs[c]
            c_scr[c, pl.ds(g + 1, 1), :] = h
            new.append(h)
        return tuple(new)

    zero = jnp.zeros((1, LANES), jnp.float32)
    lax.fori_loop(0, n_g, carry_step, (zero,) * N_SLABS, unroll=8)

    k0 = math.sqrt(2.0 / math.pi)
    for c in range(N_SLABS):
        h_in = c_scr[c, 0:n_g, :]
        for r in range(SUBLANES):
            h = hl_scr[c, r] + ac_scr[c, r] * h_in
            gb = gb_ref[0, c, pl.ds(r, n_g, stride=SUBLANES), :]
            gate = gb * (0.5 + 0.5 * jnp.tanh(gb * (k0 + (k0 * 0.044715) * (gb * gb))))
            o_ref[0, c, pl.ds(r, n_g, stride=SUBLANES), :] = h * gate


def _rglru(proj, cw, cb, w_pair, b_a, b_x, lam, *, batch, seq):
    blk = (1, N_SLABS, seq, LANES)
    n_g = seq // SUBLANES
    return pl.pallas_call(
        functools.partial(_rglru_body, seq=seq),
        grid=(batch,),
        in_specs=[pl.BlockSpec(blk, lambda b: (b, 3, 0, 0)),
                  pl.BlockSpec(blk, lambda b: (b, 4, 0, 0)),
                  _const_spec((N_SLABS, CONV_WIDTH, LANES)), _const_spec((N_SLABS, 1, LANES)),
                  _const_spec((N_SLABS, LANES, 2 * LANES)),
                  _const_spec((N_SLABS, 1, LANES)), _const_spec((N_SLABS, 1, LANES)),
                  _const_spec((N_SLABS, 1, LANES))],
        out_specs=pl.BlockSpec(blk, lambda b: (b, 0, 0, 0)),
        out_shape=jax.ShapeDtypeStruct((batch, N_SLABS, seq, LANES), jnp.float32),
        scratch_shapes=[pltpu.VMEM((N_SLABS, SUBLANES, n_g, LANES), jnp.float32),
                        pltpu.VMEM((N_SLABS, SUBLANES, n_g, LANES), jnp.float32),
                        pltpu.VMEM((N_SLABS, n_g + SUBLANES, LANES), jnp.float32)],
        compiler_params=pltpu.CompilerParams(dimension_semantics=("arbitrary",),
                                             vmem_limit_bytes=VMEM_LIMIT),
        name="rglru",
    )(proj, proj, cw, cb, w_pair, b_a, b_x, lam)


def _rope_halves_apart(w):
    lead = w.shape[:-1]
    w = w.reshape(*lead, N_HEADS // QUAD, QUAD, 2, HALF)
    return jnp.swapaxes(w, -3, -2).reshape(*lead, D_ATTN)


def _rope_tables(seq):
    pos = jnp.arange(seq, dtype=jnp.float32)
    inv = ROPE_THETA ** (-jnp.arange(0, HEAD_DIM, 2, dtype=jnp.float32) / HEAD_DIM)
    ang = pos[:, None] * inv[None, :]
    return jnp.tile(jnp.cos(ang), (1, QUAD)), jnp.tile(jnp.sin(ang), (1, QUAD))


def _pair_block_diag(w):
    z = jnp.zeros_like(w[:, 0::2])
    top = jnp.concatenate([w[:, 0::2], z], axis=3)
    bot = jnp.concatenate([z, w[:, 1::2]], axis=3)
    return jnp.concatenate([top, bot], axis=2)


def kernel(x, ffn1_norm, ffn1_w_in, ffn1_w_out, mix_norm, w_in, conv_w, conv_b, rg_w_a, rg_b_a, rg_w_x,
           rg_b_x, rg_lambda, attn_out_norm, rec_out_norm, w_out, ffn2_norm, ffn2_w_in, ffn2_w_out,
           final_norm):
    batch, seq, d = x.shape
    depth = ffn1_norm.shape[0]
    cos_t, sin_t = _rope_tables(seq)
    h = x.reshape(batch * seq, d)
    w_in = jnp.concatenate([_rope_halves_apart(w_in[..., :D_ATTN]),
                            _rope_halves_apart(w_in[..., D_ATTN:2 * D_ATTN]), w_in[..., 2 * D_ATTN:]], axis=-1)
    w_pair = jnp.concatenate([_pair_block_diag(rg_w_a), _pair_block_diag(rg_w_x)], axis=3).astype(MXU_DTYPE)
    ffn1_w = (ffn1_w_in[0].astype(MXU_DTYPE), ffn1_w_out[0].astype(MXU_DTYPE))
    w_proj = w_in[0].astype(MXU_DTYPE)
    for l in range(depth):
        def slabs(v):
            return v.reshape(N_SLABS, 1, LANES)

        h, conv = _ffn(h, ffn1_norm[l], *ffn1_w, casts=[(ffn2_w_in, l), (ffn2_w_out, l), (w_out, l)])
        ffn2_w, w_mix = conv[:2], conv[2]
        next_casts = [(ffn1_w_in, l + 1), (ffn1_w_out, l + 1), (w_in, l + 1)] if l + 1 < depth else []
        proj, conv = _inproj(h, mix_norm[l], w_proj, cos_t, sin_t, batch=batch, seq=seq, casts=next_casts)
        if next_casts:
            ffn1_w, w_proj = conv[:2], conv[2]
        y_attn = _attention(proj, batch=batch, seq=seq)
        y_rec = _rglru(proj, conv_w[l].reshape(CONV_WIDTH, N_SLABS, LANES).transpose(1, 0, 2),
                       slabs(conv_b[l]), w_pair[l], slabs(rg_b_a[l]), slabs(rg_b_x[l]),
                       slabs(rg_lambda[l]), batch=batch, seq=seq)
        h, _ = _ffn(h, ffn2_norm[l], *ffn2_w,
                    mix=(y_attn, y_rec, attn_out_norm[l], rec_out_norm[l], w_mix),
                    seq=seq, final_g=final_norm if l == depth - 1 else None)
    return h.reshape(batch, seq, d)
```

```python
import functools
import math

import jax
import jax.numpy as jnp
from jax import lax
from jax.experimental import pallas as pl
from jax.experimental.pallas import tpu as pltpu

HEAD_DIM = 64
HALF = HEAD_DIM // 2
N_HEADS = 8
D_ATTN = N_HEADS * HEAD_DIM
D_REC = 512
N_REC_BLOCKS = 8
REC_BLOCK = D_REC // N_REC_BLOCKS
CONV_WIDTH = 4
LRU_C = 8.0
ROPE_THETA = 10000.0
EPS = 1e-6
STEPS = 128
DILATIONS = (1, 4, 16)

LANES = 128
SUBLANES = 8
N_SLABS = D_ATTN // LANES
MXU_TILE = 256
QUAD = 4
VMEM_LIMIT = 56 * 1024 * 1024
VMEM_LIMIT_MAX = 60 * 1024 * 1024

MXU_DTYPE = jnp.bfloat16
NEG_BIG = -1e30


def _mm(a, b):
    return jnp.dot(a.astype(MXU_DTYPE), b.astype(MXU_DTYPE), preferred_element_type=jnp.float32)


def _mm_nt(a, b):
    return lax.dot_general(a.astype(MXU_DTYPE), b.astype(MXU_DTYPE), (((1,), (1,)), ((), ())),
                           preferred_element_type=jnp.float32)


LOG2_E = math.log2(math.e)


def _sigmoid(x):
    return 1.0 / (1.0 + jnp.exp2(x * -LOG2_E))


def _rms(x, g):
    var = jnp.mean(x * x, axis=-1, keepdims=True)
    return x * lax.rsqrt(var + EPS) * g


def _const_spec(shape):
    return pl.BlockSpec(shape, lambda *_: (0,) * len(shape), pipeline_mode=pl.Buffered(1))


def _ffn_chunks(d_ff, n_chunks):
    tiles = d_ff // MXU_TILE
    assert tiles * MXU_TILE == d_ff
    sizes = [(tiles // n_chunks + (1 if i < tiles % n_chunks else 0)) * MXU_TILE for i in range(n_chunks)]
    starts = [sum(sizes[:i]) for i in range(n_chunks)]
    return list(zip(starts, sizes))


def _mixer_out(ya_ref, yr_ref, ga_ref, gr_ref, w_ref):
    ya = jnp.concatenate([ya_ref[0, j] for j in range(N_SLABS)], axis=1)
    yr = jnp.concatenate([yr_ref[0, j] for j in range(N_SLABS)], axis=1)
    merged = jnp.concatenate([_rms(ya, ga_ref[...]), _rms(yr, gr_ref[...])], axis=1)
    return _mm(merged, w_ref[...])


def _ffn_body(*refs, d_ff, chunks, mix, final):
    refs = list(refs)
    x = refs.pop(0)[...]
    if mix:
        x = x + _mixer_out(*refs[:5])
        del refs[:5]
    g_ref, win_ref, wout_ref = refs[:3]
    fg_ref = refs[3] if final else None
    o_ref = refs[-1]
    xn = _rms(x, g_ref[...]).astype(MXU_DTYPE)
    acc = None
    for lo, size in chunks:
        gate = _mm(xn, win_ref[:, lo:lo + size])
        up = _mm(xn, win_ref[:, d_ff + lo:d_ff + lo + size])
        act = (gate * _sigmoid(gate) * up).astype(MXU_DTYPE)
        part = _mm(act, wout_ref[lo:lo + size, :])
        acc = part if acc is None else acc + part
    y = x + 0.5 * acc
    if final:
        y = _rms(y, fg_ref[...])
    o_ref[...] = y


def _ffn(x, g, w_in, w_out, *, mix=None, seq=None, final_g=None, tm=1024):
    t, d = x.shape
    d_ff = w_out.shape[0]
    final = final_g is not None
    chunks = _ffn_chunks(d_ff, 4 if mix is not None else 2)
    vmem_limit = VMEM_LIMIT_MAX if mix is not None else VMEM_LIMIT
    in_specs = [pl.BlockSpec((tm, d), lambda i: (i, 0))]
    args = [x]
    if mix is not None:
        y_attn, y_rec, g_attn, g_rec, w_mix = mix
        n_seq = seq // tm
        yblk = pl.BlockSpec((1, N_SLABS, tm, LANES), lambda i: (i // n_seq, 0, i % n_seq, 0))
        in_specs += [yblk, yblk, _const_spec((1, D_ATTN)), _const_spec((1, D_REC)),
                     _const_spec((D_ATTN + D_REC, d))]
        args += [y_attn, y_rec, g_attn.reshape(1, -1), g_rec.reshape(1, -1), w_mix]
    in_specs += [_const_spec((1, d)), _const_spec((d, 2 * d_ff)), _const_spec((d_ff, d))]
    args += [g.reshape(1, d), w_in, w_out]
    if final:
        in_specs.append(_const_spec((1, d)))
        args.append(final_g.reshape(1, d))
    return pl.pallas_call(
        functools.partial(_ffn_body, d_ff=d_ff, chunks=chunks, mix=mix is not None, final=final),
        grid=(t // tm,),
        in_specs=in_specs,
        out_specs=pl.BlockSpec((tm, d), lambda i: (i, 0)),
        out_shape=jax.ShapeDtypeStruct((t, d), jnp.float32),
        compiler_params=pltpu.CompilerParams(dimension_semantics=("arbitrary",),
                                             vmem_limit_bytes=vmem_limit),
        name=("mix_ffn" if mix is not None else "ffn") + ("_final" if final else ""),
    )(*args)


N_PROJ_SLABS = 5 * N_SLABS


def _inproj_body(x_ref, g_ref, w_ref, cos_ref, sin_ref, o_ref):
    xn = _rms(x_ref[...], g_ref[...]).astype(MXU_DTYPE)
    proj = _mm(xn, w_ref[...])
    c = cos_ref[...]
    s = sin_ref[...]

    def slab(j):
        return proj[:, j * LANES:(j + 1) * LANES]

    for base, scale in ((0, math.log2(math.e) / math.sqrt(HEAD_DIM)), (N_SLABS, 1.0)):
        for g in range(N_SLABS // 2):
            t1 = slab(base + 2 * g)
            t2 = slab(base + 2 * g + 1)
            o_ref[0, base + 2 * g] = (t1 * c - t2 * s) * scale
            o_ref[0, base + 2 * g + 1] = (t2 * c + t1 * s) * scale
    for j in range(2 * N_SLABS, N_PROJ_SLABS):
        o_ref[0, j] = slab(j)


def _inproj(x, g, w, cos_t, sin_t, *, batch, seq, tm=1024):
    t, d = x.shape
    n_seq = seq // tm
    return pl.pallas_call(
        _inproj_body,
        grid=(t // tm,),
        in_specs=[pl.BlockSpec((tm, d), lambda i: (i, 0)),
                  _const_spec((1, d)), _const_spec((d, N_PROJ_SLABS * LANES)),
                  pl.BlockSpec((tm, LANES), lambda i: (i % n_seq, 0)),
                  pl.BlockSpec((tm, LANES), lambda i: (i % n_seq, 0))],
        out_specs=pl.BlockSpec((1, N_PROJ_SLABS, tm, LANES), lambda i: (i // n_seq, 0, i % n_seq, 0)),
        out_shape=jax.ShapeDtypeStruct((batch, N_PROJ_SLABS, seq, LANES), jnp.float32),
        compiler_params=pltpu.CompilerParams(dimension_semantics=("arbitrary",),
                                             vmem_limit_bytes=VMEM_LIMIT),
        name="inproj",
    )(x, g.reshape(1, d), w, cos_t, sin_t)


def _rows(start, n, stride=1):
    return pl.ds(start, n) if stride == 1 else pl.ds(start, n, stride=stride)


def _load2(ref, lead, rows):
    return jnp.concatenate([ref[(lead, 0, rows, slice(None))], ref[(lead, 1, rows, slice(None))]], axis=1)


def _store2(ref, lead, rows, val):
    ref[(lead, 0, rows, slice(None))] = val[:, :LANES]
    ref[(lead, 1, rows, slice(None))] = val[:, LANES:]


MID_DIL = DILATIONS[1]
SUB_DIL = DILATIONS[2] // MID_DIL
CLS_SLOTS = 4


def _attn_body(q_ref, k_ref, v_ref, *rest, seq, n_casts):
    cast_in, o_ref, cast_out = rest[:n_casts], rest[n_casts], rest[n_casts + 1:2 * n_casts + 1]
    bias_scr, nat_scr, q_all, kv_all, far_all = rest[2 * n_casts + 1:]
    for src, dst in zip(cast_in, cast_out):
        dst[...] = src[0].astype(dst.dtype)
    w2 = 2 * LANES
    cls_len = seq // MID_DIL
    cls_blocks = cls_len // STEPS
    assert seq == DILATIONS[2] * STEPS and DILATIONS[0] == 1
    lane = lax.broadcasted_iota(jnp.int32, (SUBLANES, w2), 1)
    q_head = (lane % LANES) // HALF
    upper_head = lax.broadcasted_iota(jnp.int32, (SUBLANES, LANES), 1) >= HEAD_DIM
    groups = STEPS // SUBLANES

    def stack_heads(per_head):
        return jnp.concatenate([per_head(h, slice(a * SUBLANES, (a + 1) * SUBLANES))
                                for a in range(groups) for h in range(QUAD)], axis=0)

    def merge_heads(full):
        wide = full.shape[1] == w2
        tiles = []
        for a in range(groups):
            halves = []
            for t in range(2):
                lanes = slice(t * LANES, (t + 1) * LANES) if wide else slice(None)
                row0 = (a * QUAD + 2 * t) * SUBLANES
                halves.append(jnp.where(upper_head, full[row0 + SUBLANES:row0 + 2 * SUBLANES, lanes],
                                        full[row0:row0 + SUBLANES, lanes]))
            tiles.append(jnp.concatenate(halves, axis=1))
        return jnp.concatenate(tiles, axis=0)

    @pl.when((pl.program_id(0) == 0) & (pl.program_id(1) == 0))
    def _():
        def table(width, fn):
            i = lax.broadcasted_iota(jnp.int32, (STEPS, width), 0)
            j = lax.broadcasted_iota(jnp.int32, (STEPS, width), 1)
            b = fn(i, j)
            return stack_heads(lambda h, rows: b[rows])

        def cls_bias(i, j):
            d = (cls_len - STEPS) + i - j
            in_mid = (d >= 0) & (d <= STEPS)
            in_far = (d >= 0) & (d % SUB_DIL == 0)
            return jnp.where(in_mid & in_far, 1.0, jnp.where(in_mid | in_far, 0.0, NEG_BIG))

        def band_bias(i, j):
            d = j - i
            return jnp.where((d >= 0) & (d <= STEPS), 0.0, NEG_BIG)

        bias_scr[:, 0:cls_len] = table(cls_len, cls_bias)
        bias_scr[:, cls_len:cls_len + 2 * STEPS] = table(2 * STEPS, band_bias)
        bias_scr[:, cls_len + 2 * STEPS:cls_len + 3 * STEPS] = table(STEPS, lambda i, j: jnp.where(j <= i, 0.0, NEG_BIG))

    def unit(qb, kw, vw, b):
        qs = stack_heads(lambda h, rows: jnp.where(q_head == h, qb[rows], 0.0))
        sc = _mm_nt(qs, kw) + b
        m = jnp.max(sc, axis=1, keepdims=True)
        p = jnp.exp2(sc - m)
        s = jnp.sum(p, axis=1, keepdims=True)
        ov = _mm(p, vw)
        return (merge_heads(ov), merge_heads(jnp.broadcast_to(m, (QUAD * STEPS, LANES))),
                merge_heads(jnp.broadcast_to(s, (QUAD * STEPS, LANES))))

    def emit(scr, rows, res):
        for kind in range(3):
            _store2(scr, kind, rows, res[kind])

    near_lo = cls_len
    emit(nat_scr, _rows(0, STEPS),
         unit(_load2(q_ref, 0, _rows(0, STEPS)), _load2(k_ref, 0, _rows(0, STEPS)),
              _load2(v_ref, 0, _rows(0, STEPS)), bias_scr[:, near_lo + 2 * STEPS:near_lo + 3 * STEPS]))

    def near_block(nb, carry):
        q0 = pl.multiple_of(nb * STEPS, STEPS)
        k_rows = _rows(pl.multiple_of(q0 - STEPS, STEPS), 2 * STEPS)
        emit(nat_scr, _rows(q0, STEPS),
             unit(_load2(q_ref, 0, _rows(q0, STEPS)), _load2(k_ref, 0, k_rows), _load2(v_ref, 0, k_rows),
                  bias_scr[:, near_lo:near_lo + 2 * STEPS]))
        return carry

    lax.fori_loop(1, seq // STEPS, near_block, 0, unroll=True)

    def klass(c, slot):
        q_scr, kv_scr, far_scr = q_all.at[slot], kv_all.at[slot], far_all.at[slot]
        cls_rows = _rows(c, cls_len, MID_DIL)
        _store2(q_scr, 0, _rows(0, cls_len), _load2(q_ref, 0, cls_rows))
        for i, ref in enumerate((k_ref, v_ref)):
            _store2(kv_scr, i, _rows(0, cls_len), _load2(ref, 0, cls_rows).astype(MXU_DTYPE))
        for nb in range(cls_blocks):
            k_rows = _rows(0, (nb + 1) * STEPS)
            emit(far_scr, _rows(nb * STEPS, STEPS),
                 unit(_load2(q_scr, 0, _rows(nb * STEPS, STEPS)), _load2(kv_scr, 0, k_rows),
                      _load2(kv_scr, 1, k_rows), bias_scr[:, (cls_blocks - 1 - nb) * STEPS:cls_len]))

    assert CLS_SLOTS == MID_DIL
    for c in range(MID_DIL):
        klass(c, c)

    def merge(c, carry):
        far_scr = far_all.at[c]
        for nb in range(cls_blocks):
            dense = _rows(nb * STEPS, STEPS)
            nat = _rows(c + nb * (MID_DIL * STEPS), STEPS, MID_DIL)
            for sl in range(2):
                m_near, m_far = nat_scr[1, sl, nat, :], far_scr[1, sl, dense, :]
                m_all = jnp.maximum(m_near, m_far)
                w_near, w_far = jnp.exp2(m_near - m_all), jnp.exp2(m_far - m_all)
                numer = nat_scr[0, sl, nat, :] * w_near + far_scr[0, sl, dense, :] * w_far
                denom = nat_scr[2, sl, nat, :] * w_near + far_scr[2, sl, dense, :] * w_far
                o_ref[0, sl, nat, :] = numer / denom
        return carry

    lax.fori_loop(0, MID_DIL, merge, 0)


def _attention(proj, casts, *, batch, seq):
    n_quads = N_HEADS // QUAD
    n_steps = batch * n_quads
    blk = (1, 2, seq, LANES)
    nat = pltpu.VMEM((3, 2, seq, LANES), jnp.float32)
    cls_q = pltpu.VMEM((CLS_SLOTS, 1, 2, seq // MID_DIL, LANES), jnp.float32)
    cls_kv = pltpu.VMEM((CLS_SLOTS, 2, 2, seq // MID_DIL, LANES), MXU_DTYPE)
    cls_out = pltpu.VMEM((CLS_SLOTS, 3, 2, seq // MID_DIL, LANES), jnp.float32)
    cast_in_specs, cast_out_specs, cast_out_shapes = [], [], []
    for w, layer in casts:
        _, rows, cols = w.shape
        share = next(k for k in (1, 2, 4, 8) if rows * k % n_steps == 0 and (rows * k // n_steps) % (2 * SUBLANES) == 0)
        blk_rows = rows * share // n_steps
        cast_in_specs.append(pl.BlockSpec(
            (1, blk_rows, cols), lambda b, g, layer=layer, share=share: (layer, (b * n_quads + g) // share, 0)))
        cast_out_specs.append(pl.BlockSpec(
            (blk_rows, cols), lambda b, g, share=share: ((b * n_quads + g) // share, 0)))
        cast_out_shapes.append(jax.ShapeDtypeStruct((rows, cols), MXU_DTYPE))
    outs = pl.pallas_call(
        functools.partial(_attn_body, seq=seq, n_casts=len(casts)),
        grid=(batch, n_quads),
        in_specs=[pl.BlockSpec(blk, lambda b, g: (b, g, 0, 0)),
                  pl.BlockSpec(blk, lambda b, g: (b, n_quads + g, 0, 0)),
                  pl.BlockSpec(blk, lambda b, g: (b, 2 * n_quads + g, 0, 0))] + cast_in_specs,
        out_specs=[pl.BlockSpec(blk, lambda b, g: (b, g, 0, 0))] + cast_out_specs,
        out_shape=[jax.ShapeDtypeStruct((batch, N_SLABS, seq, LANES), jnp.float32)] + cast_out_shapes,
        scratch_shapes=[pltpu.VMEM((QUAD * STEPS, seq // MID_DIL + 3 * STEPS), jnp.float32), nat,
                        cls_q, cls_kv, cls_out],
        compiler_params=pltpu.CompilerParams(dimension_semantics=("arbitrary", "arbitrary"),
                                             vmem_limit_bytes=VMEM_LIMIT),
        name="dilated_attn",
    )(proj, proj, proj, *[w for w, _ in casts])
    return outs[0], outs[1:]


def _rglru_body(xb_ref, gb_ref, cw_ref, cb_ref, w_ref, ba_ref, bx_ref, lam_ref, o_ref,
                hl_scr, ac_scr, c_scr, *, seq):
    n_g = seq // SUBLANES
    row = lax.broadcasted_iota(jnp.int32, (n_g, LANES), 0)

    def shift_down(a):
        return jnp.where(row == 0, 0.0, pltpu.roll(a, 1, axis=0))

    for c in range(N_SLABS):
        xs = [xb_ref[0, c, pl.ds(r, n_g, stride=SUBLANES), :] for r in range(SUBLANES)]
        prev = [shift_down(xs[SUBLANES - k]) for k in range(1, CONV_WIDTH)]

        def stream(r):
            return xs[r] if r >= 0 else prev[-r - 1]

        cw = cw_ref[c]
        xr = []
        for r in range(SUBLANES):
            acc = cb_ref[c]
            for j in range(CONV_WIDTH):
                acc = acc + cw[j:j + 1, :] * stream(r - (CONV_WIDTH - 1) + j)
            xr.append(acc)
        y = _mm(jnp.concatenate(xr, axis=0), w_ref[c])
        z = -lam_ref[c]
        decay = LRU_C * (jnp.maximum(z, 0.0) + jnp.log(1.0 + jnp.exp(-jnp.abs(z))))
        hl = None
        ac = None
        for r in range(SUBLANES):
            yr = y[r * n_g:(r + 1) * n_g]
            r_gate = _sigmoid(yr[:, :LANES] + ba_ref[c])
            i_gate = _sigmoid(yr[:, LANES:] + bx_ref[c])
            neg_log_a = r_gate * decay
            a = jnp.exp2(neg_log_a * -LOG2_E)
            u = jnp.sqrt(jnp.tanh(neg_log_a) * (a * a + 1.0)) * (i_gate * xr[r])
            hl = u if r == 0 else a * hl + u
            ac = a if r == 0 else a * ac
            hl_scr[c, r] = hl
            ac_scr[c, r] = ac
        c_scr[c, 0:SUBLANES, :] = jnp.zeros((SUBLANES, LANES), jnp.float32)

    last = SUBLANES - 1

    def carry_step(g, hs):
        new = []
        for c in range(N_SLABS):
            h = hl_scr[c, last, pl.ds(g, 1), :] + ac_scr[c, last, pl.ds(g, 1), :] * hs[c]
            c_scr[c, pl.ds(g + 1, 1), :] = h
            new.append(h)
        return tuple(new)

    zero = jnp.zeros((1, LANES), jnp.float32)
    lax.fori_loop(0, n_g, carry_step, (zero,) * N_SLABS, unroll=8)

    k0 = math.sqrt(2.0 / math.pi)
    for c in range(N_SLABS):
        h_in = c_scr[c, 0:n_g, :]
        for r in range(SUBLANES):
            h = hl_scr[c, r] + ac_scr[c, r] * h_in
            gb = gb_ref[0, c, pl.ds(r, n_g, stride=SUBLANES), :]
            gate = gb * (0.5 + 0.5 * jnp.tanh(gb * (k0 + (k0 * 0.044715) * (gb * gb))))
            o_ref[0, c, pl.ds(r, n_g, stride=SUBLANES), :] = h * gate


def _rglru(proj, cw, cb, w_pair, b_a, b_x, lam, *, batch, seq):
    blk = (1, N_SLABS, seq, LANES)
    n_g = seq // SUBLANES
    return pl.pallas_call(
        functools.partial(_rglru_body, seq=seq),
        grid=(batch,),
        in_specs=[pl.BlockSpec(blk, lambda b: (b, 3, 0, 0)),
                  pl.BlockSpec(blk, lambda b: (b, 4, 0, 0)),
                  _const_spec((N_SLABS, CONV_WIDTH, LANES)), _const_spec((N_SLABS, 1, LANES)),
                  _const_spec((N_SLABS, LANES, 2 * LANES)),
                  _const_spec((N_SLABS, 1, LANES)), _const_spec((N_SLABS, 1, LANES)),
                  _const_spec((N_SLABS, 1, LANES))],
        out_specs=pl.BlockSpec(blk, lambda b: (b, 0, 0, 0)),
        out_shape=jax.ShapeDtypeStruct((batch, N_SLABS, seq, LANES), jnp.float32),
        scratch_shapes=[pltpu.VMEM((N_SLABS, SUBLANES, n_g, LANES), jnp.float32),
                        pltpu.VMEM((N_SLABS, SUBLANES, n_g, LANES), jnp.float32),
                        pltpu.VMEM((N_SLABS, n_g + SUBLANES, LANES), jnp.float32)],
        compiler_params=pltpu.CompilerParams(dimension_semantics=("arbitrary",),
                                             vmem_limit_bytes=VMEM_LIMIT),
        name="rglru",
    )(proj, proj, cw, cb, w_pair, b_a, b_x, lam)


def _rope_halves_apart(w):
    d = w.shape[0]
    w = w.reshape(d, N_HEADS // QUAD, QUAD, 2, HALF)
    return w.transpose(0, 1, 3, 2, 4).reshape(d, D_ATTN)


def _rope_tables(seq):
    pos = jnp.arange(seq, dtype=jnp.float32)
    inv = ROPE_THETA ** (-jnp.arange(0, HEAD_DIM, 2, dtype=jnp.float32) / HEAD_DIM)
    ang = pos[:, None] * inv[None, :]
    return jnp.tile(jnp.cos(ang), (1, QUAD)), jnp.tile(jnp.sin(ang), (1, QUAD))


def _pair_block_diag(w):
    z = jnp.zeros((N_SLABS, REC_BLOCK, REC_BLOCK), w.dtype)
    top = jnp.concatenate([w[0::2], z], axis=2)
    bot = jnp.concatenate([z, w[1::2]], axis=2)
    return jnp.concatenate([top, bot], axis=1)


def kernel(x, ffn1_norm, ffn1_w_in, ffn1_w_out, mix_norm, w_in, conv_w, conv_b, rg_w_a, rg_b_a, rg_w_x,
           rg_b_x, rg_lambda, attn_out_norm, rec_out_norm, w_out, ffn2_norm, ffn2_w_in, ffn2_w_out,
           final_norm):
    batch, seq, d = x.shape
    depth = ffn1_norm.shape[0]
    cos_t, sin_t = _rope_tables(seq)
    h = x.reshape(batch * seq, d)
    ffn1_w = (ffn1_w_in[0].astype(MXU_DTYPE), ffn1_w_out[0].astype(MXU_DTYPE))
    w_proj = w_in[0].astype(MXU_DTYPE)
    for l in range(depth):
        w_proj = jnp.concatenate([_rope_halves_apart(w_proj[:, :D_ATTN]),
                                  _rope_halves_apart(w_proj[:, D_ATTN:2 * D_ATTN]), w_proj[:, 2 * D_ATTN:]], axis=1)
        w_pair = jnp.concatenate([_pair_block_diag(rg_w_a[l]), _pair_block_diag(rg_w_x[l])], axis=2)

        def slabs(v):
            return v.reshape(N_SLABS, 1, LANES)

        h = _ffn(h, ffn1_norm[l], *ffn1_w)
        proj = _inproj(h, mix_norm[l], w_proj, cos_t, sin_t, batch=batch, seq=seq)
        casts = [(ffn2_w_in, l), (ffn2_w_out, l), (w_out, l)]
        if l + 1 < depth:
            casts += [(ffn1_w_in, l + 1), (ffn1_w_out, l + 1), (w_in, l + 1)]
        y_attn, conv = _attention(proj, casts, batch=batch, seq=seq)
        y_rec = _rglru(proj, conv_w[l].reshape(CONV_WIDTH, N_SLABS, LANES).transpose(1, 0, 2),
                       slabs(conv_b[l]), w_pair.astype(MXU_DTYPE), slabs(rg_b_a[l]), slabs(rg_b_x[l]),
                       slabs(rg_lambda[l]), batch=batch, seq=seq)
        h = _ffn(h, ffn2_norm[l], conv[0], conv[1],
                 mix=(y_attn, y_rec, attn_out_norm[l], rec_out_norm[l], conv[2]),
                 seq=seq, final_g=final_norm if l == depth - 1 else None)
        if l + 1 < depth:
            ffn1_w = (conv[3], conv[4])
            w_proj = conv[5]
    return h.reshape(batch, seq, d)
```
